```python
import math
import jax, jax.numpy as jnp
from jax import lax
import numpy as np

D_MODEL = 1024
BATCH = 4
SEQ = 8192
DEPTH = 2

GRID_W = 64
CTX_LEN = 256
HEAD_DIM = 64
ROPE_BASE = 10000.0
NORM_EPS = 1e-6
NEG_INF = -1e30
F32 = jnp.float32

A_HEADS = 4
A_KV_HEADS = 2
A_GROUP = A_HEADS // A_KV_HEADS
A_WINDOW = 128
A_BLOCK = 128
B_HEADS = 4
B_QK_DIM = 64
B_V_DIM = 2 * B_QK_DIM
B_BLOCK = 128
C_HEADS = 4
C_QK_DIM = 32
C_V_DIM = 2 * C_QK_DIM
C_CHUNK = 128

A_Q = A_HEADS * HEAD_DIM
A_KV = A_KV_HEADS * HEAD_DIM
B_QK = B_HEADS * 2 * B_QK_DIM
B_V = B_HEADS * B_V_DIM
C_QK = C_HEADS * C_QK_DIM
C_V = C_HEADS * C_V_DIM
SPLIT_SIZES = (A_Q, A_KV, A_KV, B_QK, B_QK, B_V, C_QK, C_QK, C_V, C_V)
IN_WIDTH = sum(SPLIT_SIZES)
SPLIT_POINTS = tuple(int(s) for s in np.cumsum(SPLIT_SIZES)[:-1])
MIX_WIDTH = A_Q + B_V + C_V

D_FF = 2816
CONV_W = 3
N_MOD = 6

kernel_name = "hybrid_parallel_groups_diffusion_trunk"


def rms_norm(x, g):
    xf = x.astype(F32)
    y = xf * lax.rsqrt(jnp.mean(xf * xf, axis=-1, keepdims=True) + NORM_EPS)
    return (y * g.astype(F32)).astype(x.dtype)


def modulate(h, shift, scale):
    return h * (1.0 + scale) + shift


def axial_rope_tables(n_tok, dim):
    n_rows = n_tok // GRID_W
    rows = jnp.repeat(jnp.arange(n_rows, dtype=F32), GRID_W)
    cols = jnp.tile(jnp.arange(GRID_W, dtype=F32), n_rows)
    n_freq = dim // 4
    inv = ROPE_BASE ** (-jnp.arange(n_freq, dtype=F32) / n_freq)
    ang = jnp.concatenate([rows[:, None] * inv, cols[:, None] * inv], axis=-1)
    return jnp.cos(ang), jnp.sin(ang)


def apply_rope(x, cos, sin):
    half = x.shape[-1] // 2
    shp = cos.shape[:1] + (1,) * (x.ndim - 3) + cos.shape[1:]
    c = cos.reshape(shp).astype(x.dtype)
    s = sin.reshape(shp).astype(x.dtype)
    x1, x2 = x[..., :half], x[..., half:]
    return jnp.concatenate([x1 * c - x2 * s, x1 * s + x2 * c], axis=-1)


def split_projection(p):
    bsz, n = p.shape[:2]
    qa, ka, va, qb, kb, vb, qr, kr, vr, gr = jnp.split(p, SPLIT_POINTS, axis=-1)
    qa = qa.reshape(bsz, n, A_HEADS, HEAD_DIM)
    ka = ka.reshape(bsz, n, A_KV_HEADS, HEAD_DIM)
    va = va.reshape(bsz, n, A_KV_HEADS, HEAD_DIM)
    qb = qb.reshape(bsz, n, B_HEADS, 2, B_QK_DIM)
    kb = kb.reshape(bsz, n, B_HEADS, 2, B_QK_DIM)
    vb = vb.reshape(bsz, n, B_HEADS, B_V_DIM)
    qr = qr.reshape(bsz, n, C_HEADS, C_QK_DIM)
    kr = kr.reshape(bsz, n, C_HEADS, C_QK_DIM) * (C_QK_DIM ** -0.5)
    vr = vr.reshape(bsz, n, C_HEADS, C_V_DIM)
    return qa, ka, va, qb, kb, vb, qr, kr, vr, gr


def softmax_with_sink(scores, sink):
    m = jnp.maximum(jnp.max(scores, axis=-1, keepdims=True), sink)
    e = jnp.exp(scores - m)
    return e / (jnp.sum(e, axis=-1, keepdims=True) + jnp.exp(sink - m))


def window_gqa_latent(q, k, v, kc, vc, sink):
    bsz, n, _, d = q.shape
    nb = n // A_BLOCK
    scale = d ** -0.5
    qb = q.reshape(bsz, nb, A_BLOCK, A_KV_HEADS, A_GROUP, d)
    pad = ((0, 0), (A_BLOCK, A_BLOCK), (0, 0), (0, 0))
    kp = jnp.pad(k, pad).reshape(bsz, nb + 2, A_BLOCK, A_KV_HEADS, d)
    vp = jnp.pad(v, pad).reshape(bsz, nb + 2, A_BLOCK, A_KV_HEADS, d)
    kband = jnp.concatenate([kp[:, :-2], kp[:, 1:-1], kp[:, 2:]], axis=2)
    vband = jnp.concatenate([vp[:, :-2], vp[:, 1:-1], vp[:, 2:]], axis=2)
    s_loc = jnp.einsum('bnqhgd,bnkhd->bhgnqk', qb, kband).astype(F32) * scale
    s_ctx = jnp.einsum('bnqhgd,blhd->bhgnql', qb, kc).astype(F32) * scale
    qi = jnp.arange(A_BLOCK)[:, None]
    kj = jnp.arange(3 * A_BLOCK)[None, :]
    rel = kj - A_BLOCK - qi
    kpos = jnp.arange(nb)[:, None, None] * A_BLOCK - A_BLOCK + kj
    valid = (jnp.abs(rel) <= A_WINDOW)[None] & (kpos >= 0) & (kpos < n)
    s_loc = jnp.where(valid, s_loc, NEG_INF)
    scores = jnp.concatenate([s_loc, s_ctx], axis=-1)
    sink_b = sink.astype(F32).reshape(A_KV_HEADS, A_GROUP)[None, :, :, None, None, None]
    p = softmax_with_sink(scores, sink_b).astype(v.dtype)
    p_loc, p_ctx = p[..., :3 * A_BLOCK], p[..., 3 * A_BLOCK:]
    out = (jnp.einsum('bhgnqk,bnkhd->bnqhgd', p_loc, vband)
           + jnp.einsum('bhgnql,blhd->bnqhgd', p_ctx, vc))
    return out.reshape(bsz, n, A_Q)


def gqa_context(q, k, v, sink):
    bsz, n, _, d = q.shape
    qg = q.reshape(bsz, n, A_KV_HEADS, A_GROUP, d)
    s = jnp.einsum('blhgd,bmhd->bhglm', qg, k).astype(F32) * (d ** -0.5)
    sink_b = sink.astype(F32).reshape(A_KV_HEADS, A_GROUP)[None, :, :, None, None]
    p = softmax_with_sink(s, sink_b).astype(v.dtype)
    return jnp.einsum('bhglm,bmhd->blhgd', p, v).reshape(bsz, n, A_Q)


def diff_attend(q, k, v, lam):
    s = jnp.einsum('bqhcd,bkhcd->bhcqk', q, k).astype(F32) * (B_QK_DIM ** -0.5)
    p = jax.nn.softmax(s, axis=-1)
    w = p[:, :, 0] - lam * p[:, :, 1]
    return jnp.einsum('bhqk,bkhd->bqhd', w.astype(v.dtype), v)


def diff_attention_latent(q, k, v, kc, vc, lam):
    bsz, n = q.shape[:2]
    k_all = jnp.concatenate([kc, k], axis=1)
    v_all = jnp.concatenate([vc, v], axis=1)
    qblocks = jnp.moveaxis(q.reshape(bsz, n // B_BLOCK, B_BLOCK, B_HEADS, 2, B_QK_DIM), 1, 0)
    out = lax.map(lambda qb: diff_attend(qb, k_all, v_all, lam), qblocks)
    return jnp.moveaxis(out, 0, 1).reshape(bsz, n, B_HEADS, B_V_DIM)


def diff_head_norm(o, g, lam_init):
    of = o.astype(F32)
    y = of * lax.rsqrt(jnp.mean(of * of, axis=-1, keepdims=True) + NORM_EPS)
    y = y * g.astype(F32) * (1.0 - lam_init)
    return y.reshape(o.shape[0], o.shape[1], B_V).astype(o.dtype)


def retention_chunked(q, k, v, log_gamma, state0, include_diag):
    bsz, nh, n, dk = q.shape
    dv = v.shape[-1]
    nc = n // C_CHUNK
    qc = q.astype(F32).reshape(bsz, nh, nc, C_CHUNK, dk)
    kc = k.astype(F32).reshape(bsz, nh, nc, C_CHUNK, dk)
    vc = v.astype(F32).reshape(bsz, nh, nc, C_CHUNK, dv)
    idx = jnp.arange(C_CHUNK, dtype=F32)
    lg = log_gamma.astype(F32)[:, None]
    diff = idx[:, None] - idx[None, :]
    mask = (diff >= 0) if include_diag else (diff > 0)
    decay = jnp.where(mask, jnp.exp(lg[:, :, None] * jnp.maximum(diff, 0.0)), 0.0)
    att = jnp.einsum('bhncd,bhnjd->bhncj', qc, kc) * decay[None, :, None]
    intra = jnp.einsum('bhncj,bhnje->bhnce', att, vc)
    q_decay = jnp.exp(lg * (idx + 1.0))
    k_decay = jnp.exp(lg * (C_CHUNK - 1.0 - idx))
    chunk_kv = jnp.einsum('bhnjd,bhnje->bhnde', kc * k_decay[None, :, None, :, None], vc)
    chunk_decay = jnp.exp(lg * C_CHUNK)[None, :, :, None]

    def step(state, kv):
        return chunk_decay * state + kv, state

    _, prev = lax.scan(step, state0, jnp.moveaxis(chunk_kv, 2, 0))
    prev = jnp.moveaxis(prev, 0, 2)
    cross = jnp.einsum('bhncd,bhnde->bhnce', qc * q_decay[None, :, None, :, None], prev)
    return (intra + cross).reshape(bsz, nh, n, dv)


def bidir_retention(q, k, v, lg_f, lg_b, s_f, s_b):
    y_f = retention_chunked(q, k, v, lg_f, s_f, True)
    flip = lambda t: jnp.flip(t, axis=2)
    y_b = flip(retention_chunked(flip(q), flip(k), flip(v), lg_b, s_b, False))
    return y_f + y_b


def context_final_states(k, v, lg_f, lg_b):
    n = k.shape[2]
    idx = jnp.arange(n, dtype=F32)
    wf = jnp.exp(lg_f[:, None] * (n - 1.0 - idx))
    wb = jnp.exp(lg_b[:, None] * idx)
    kf = k.astype(F32)
    vf = v.astype(F32)
    s_f = jnp.einsum('bhld,bhle->bhde', kf * wf[None, :, :, None], vf)
    s_b = jnp.einsum('bhld,bhle->bhde', kf * wb[None, :, :, None], vf)
    return s_f, s_b


def retention_out(y, gate):
    yc = y - jnp.mean(y, axis=-1, keepdims=True)
    yn = yc * lax.rsqrt(jnp.mean(yc * yc, axis=-1, keepdims=True) + NORM_EPS)
    yn = jnp.swapaxes(yn, 1, 2).reshape(gate.shape)
    return (jax.nn.silu(gate.astype(F32)) * yn).astype(gate.dtype)


def dwconv3(u, w, b):
    up = jnp.pad(u, ((0, 0), (1, 1), (0, 0)))
    return up[:, :-2] * w[0] + up[:, 1:-1] * w[1] + up[:, 2:] * w[2] + b


def gated_conv_ffn(h, w_up, conv_w, conv_b, w_down):
    u = dwconv3(h @ w_up, conv_w, conv_b)
    val, gate = jnp.split(u, 2, axis=-1)
    return (jax.nn.silu(gate) * val) @ w_down


def to_bhnd(t):
    return jnp.swapaxes(t, 1, 2)


def setup_inputs(seed: int = 0) -> dict:
    key = jax.random.key(seed)
    ks = jax.random.split(key, 20)
    nrm = lambda k, shape, s: jax.random.normal(k, shape, F32) * s
    gam = 1.0 - 2.0 ** (-5.0 - np.arange(C_HEADS))
    logit0 = jnp.asarray(np.log(gam / (1.0 - gam)).astype(np.float32))
    return {
        "x": nrm(ks[0], (BATCH, SEQ, D_MODEL), 1.0),
        "c": nrm(ks[1], (BATCH, D_MODEL), 1.0),
        "ctx": nrm(ks[2], (BATCH, CTX_LEN, D_MODEL), 1.0),
        "c_ctx": nrm(ks[3], (D_MODEL,), 1.0),
        "w_mod": nrm(ks[4], (DEPTH, D_MODEL, N_MOD * D_MODEL), D_MODEL ** -0.5),
        "b_mod": nrm(ks[5], (DEPTH, N_MOD * D_MODEL), 0.01),
        "norm1_g": 1.0 + nrm(ks[6], (DEPTH, D_MODEL), 0.01),
        "norm2_g": 1.0 + nrm(ks[7], (DEPTH, D_MODEL), 0.01),
        "w_in": nrm(ks[8], (DEPTH, D_MODEL, IN_WIDTH), D_MODEL ** -0.5),
        "w_out": nrm(ks[9], (DEPTH, MIX_WIDTH, D_MODEL), MIX_WIDTH ** -0.5),
        "attn_sink": nrm(ks[10], (DEPTH, A_HEADS), 0.5),
        "diff_lambda": nrm(ks[11], (DEPTH, 4, B_QK_DIM), 0.1),
        "diff_subln_g": 1.0 + nrm(ks[12], (DEPTH, B_V_DIM), 0.01),
        "ret_decay_logit": logit0[None, None, :] + nrm(ks[13], (DEPTH, 2, C_HEADS), 0.01),
        "w_up": nrm(ks[14], (DEPTH, D_MODEL, 2 * D_FF), D_MODEL ** -0.5),
        "conv_w": nrm(ks[15], (DEPTH, CONV_W, 2 * D_FF), CONV_W ** -0.5),
        "conv_b": nrm(ks[16], (DEPTH, 2 * D_FF), 0.01),
        "w_down": nrm(ks[17], (DEPTH, D_FF, D_MODEL), D_FF ** -0.5),
        "final_g": 1.0 + nrm(ks[18], (D_MODEL,), 0.01),
    }


def reference(x, c, ctx, c_ctx, w_mod, b_mod, norm1_g, norm2_g, w_in, w_out, attn_sink,
              diff_lambda, diff_subln_g, ret_decay_logit, w_up, conv_w, conv_b, w_down, final_g):
    bsz, n_lat = x.shape[0], x.shape[1]
    cos_h, sin_h = axial_rope_tables(n_lat, HEAD_DIM)
    cos_r, sin_r = axial_rope_tables(n_lat, C_QK_DIM)
    zero_state = jnp.zeros((bsz, C_HEADS, C_QK_DIM, C_V_DIM), F32)
    xc = ctx
    for l in range(DEPTH):
        is_last = l == DEPTH - 1
        lam_init = 0.8 - 0.6 * math.exp(-0.3 * l)
        mod = jax.nn.silu(c) @ w_mod[l] + b_mod[l]
        mod_c = jax.nn.silu(c_ctx) @ w_mod[l] + b_mod[l]
        sh1, sc1, g1, sh2, sc2, g2 = [m[:, None, :] for m in jnp.split(mod, N_MOD, axis=-1)]
        csh1, csc1, cg1, csh2, csc2, cg2 = jnp.split(mod_c, N_MOD, axis=-1)

        h = modulate(rms_norm(x, norm1_g[l]), sh1, sc1)
        hc = modulate(rms_norm(xc, norm1_g[l]), csh1, csc1)
        qa, ka, va, qb, kb, vb, qr, kr, vr, gr = split_projection(h @ w_in[l])
        qa_c, ka_c, va_c, qb_c, kb_c, vb_c, qr_c, kr_c, vr_c, gr_c = split_projection(hc @ w_in[l])

        qa, ka = apply_rope(qa, cos_h, sin_h), apply_rope(ka, cos_h, sin_h)
        qb, kb = apply_rope(qb, cos_h, sin_h), apply_rope(kb, cos_h, sin_h)
        qr, kr = apply_rope(qr, cos_r, sin_r), apply_rope(kr, cos_r, sin_r)

        lq1, lk1, lq2, lk2 = diff_lambda[l].astype(F32)
        lam = jnp.exp(jnp.sum(lq1 * lk1)) - jnp.exp(jnp.sum(lq2 * lk2)) + lam_init
        log_g = jax.nn.log_sigmoid(ret_decay_logit[l].astype(F32))
        lg_f, lg_b = log_g[0], log_g[1]
        qr_c, kr_c, vr_c = to_bhnd(qr_c), to_bhnd(kr_c), to_bhnd(vr_c)
        s_f, s_b = context_final_states(kr_c, vr_c, lg_f, lg_b)

        a_out = window_gqa_latent(qa, ka, va, ka_c, va_c, attn_sink[l])
        b_out = diff_head_norm(diff_attention_latent(qb, kb, vb, kb_c, vb_c, lam),
                               diff_subln_g[l], lam_init)
        c_out = retention_out(bidir_retention(to_bhnd(qr), to_bhnd(kr), to_bhnd(vr),
                                              lg_f, lg_b, s_f, s_b), gr)
        mix = jnp.concatenate([a_out, b_out.astype(a_out.dtype), c_out.astype(a_out.dtype)], axis=-1)
        x_new = x + g1 * (mix @ w_out[l])
        h2 = modulate(rms_norm(x_new, norm2_g[l]), sh2, sc2)
        x_new = x_new + g2 * gated_conv_ffn(h2, w_up[l], conv_w[l], conv_b[l], w_down[l])

        if not is_last:
            a_c = gqa_context(qa_c, ka_c, va_c, attn_sink[l])
            b_c = diff_head_norm(diff_attend(qb_c, kb_c, vb_c, lam), diff_subln_g[l], lam_init)
            c_c = retention_out(bidir_retention(qr_c, kr_c, vr_c, lg_f, lg_b, zero_state, zero_state),
                                gr_c)
            mix_c = jnp.concatenate([a_c, b_c.astype(a_c.dtype), c_c.astype(a_c.dtype)], axis=-1)
            xc = xc + cg1 * (mix_c @ w_out[l])
            hc2 = modulate(rms_norm(xc, norm2_g[l]), csh2, csc2)
            xc = xc + cg2 * gated_conv_ffn(hc2, w_up[l], conv_w[l], conv_b[l], w_down[l])
        x = x_new
    return rms_norm(x, final_g)
```

```python
import functools
import math

import jax
import jax.numpy as jnp
from jax import lax
from jax.experimental import pallas as pl
from jax.experimental.pallas import tpu as pltpu

F32 = jnp.float32
BF16 = jnp.bfloat16

GRID_W = 64
HEAD_DIM = 64
ROPE_BASE = 10000.0
NORM_EPS = 1e-6
NEG_INF = -1e30

A_HEADS = 4
A_KV_HEADS = 2
A_WINDOW = 128
A_BLOCK = 128
B_HEADS = 4
B_QK_DIM = 64
B_V_DIM = 128
C_HEADS = 4
C_QK_DIM = 32
C_V_DIM = 64
C_CHUNK = 128

A_Q = A_HEADS * HEAD_DIM
A_KV = A_KV_HEADS * HEAD_DIM
B_QK = B_HEADS * 2 * B_QK_DIM
B_V = B_HEADS * B_V_DIM
C_QK = C_HEADS * C_QK_DIM
C_V = C_HEADS * C_V_DIM
IN_WIDTH = A_Q + 2 * A_KV + 2 * B_QK + B_V + 2 * C_QK + 2 * C_V
N_MOD = 6
CONV_W = 3

LANES = 128
SUBLANES = 8
VMEM_LIMIT = 56 * 1024 * 1024

OFF_QA = 0
OFF_QB = A_Q + 2 * A_KV
OFF_KB = OFF_QB + B_QK
OFF_VB = OFF_KB + B_QK
OFF_QR = OFF_VB + B_V
OFF_VR = OFF_QR + 2 * C_QK
OFF_GR = OFF_VR + C_V


def _params(*sem):
    return pltpu.CompilerParams(dimension_semantics=sem, vmem_limit_bytes=VMEM_LIMIT)


def _nt_dot(a, b):
    return lax.dot_general(a, b, (((1,), (1,)), ((), ())), preferred_element_type=F32)


def _dot(a, b):
    return jnp.dot(a, b, preferred_element_type=F32)


def _rms(x):
    return x * lax.rsqrt(jnp.mean(x * x, axis=-1, keepdims=True) + NORM_EPS)


def _sigmoid(x):
    return 1.0 / (1.0 + jnp.exp(-x))


def _mod_kernel(c_ref, w_ref, b_ref, o_ref):
    c = c_ref[...]
    a = (c * _sigmoid(c)).astype(BF16)
    o_ref[0] = _dot(a, w_ref[0].astype(BF16)) + b_ref[0]


def _modulation(cc, w_mod, b_mod):
    depth, d, width = w_mod.shape
    rows = cc.shape[0]
    blk = d
    return pl.pallas_call(
        _mod_kernel,
        grid=(depth, width // blk),
        in_specs=[
            pl.BlockSpec((rows, d), lambda l, j: (0, 0)),
            pl.BlockSpec((1, d, blk), lambda l, j: (l, 0, j)),
            pl.BlockSpec((1, 1, blk), lambda l, j: (l, 0, j)),
        ],
        out_specs=pl.BlockSpec((1, rows, blk), lambda l, j: (l, 0, j)),
        out_shape=jax.ShapeDtypeStruct((depth, rows, width), F32),
        compiler_params=_params("arbitrary", "arbitrary"),
        name="modulation",
    )(cc, w_mod, b_mod.reshape(depth, 1, width))


def _rope_block(blk, cos, sin_signed, half):
    lane = lax.broadcasted_iota(jnp.int32, blk.shape, 1)
    first = (lane & (2 * half - 1)) < half
    swapped = jnp.where(first, pltpu.roll(blk, LANES - half, 1), pltpu.roll(blk, half, 1))
    return blk * cos + swapped * sin_signed


def _inproj_kernel(*refs, rope):
    if rope:
        (x_ref, g_ref, sh_ref, sc_ref, w_ref, c64_ref, s64_ref, c32_ref, s32_ref,
         qa_ref, ka_ref, va_ref, qb_ref, kb_ref, vb_ref, qr_ref, kr_ref, vr_ref, gr_ref) = refs
    else:
        (x_ref, g_ref, sh_ref, sc_ref, w_ref,
         qa_ref, ka_ref, va_ref, qb_ref, kb_ref, vb_ref, qr_ref, kr_ref, vr_ref, gr_ref) = refs
    x = x_ref[0]
    h = (_rms(x) * g_ref[...]) * (1.0 + sc_ref[0]) + sh_ref[0]
    h = h.astype(BF16)

    def rot(blk, dim):
        if not rope:
            return blk
        if dim == HEAD_DIM:
            return _rope_block(blk, c64_ref[...], s64_ref[...], HEAD_DIM // 2)
        return _rope_block(blk, c32_ref[...], s32_ref[...], C_QK_DIM // 2)

    a_scale = HEAD_DIM ** -0.5
    b_scale = B_QK_DIM ** -0.5
    c_scale = C_QK_DIM ** -0.5

    p = _dot(h, w_ref[:, OFF_QA:OFF_QB])
    for j in range(A_Q // LANES):
        qa_ref[0, :, j * LANES:(j + 1) * LANES] = rot(
            p[:, j * LANES:(j + 1) * LANES] * a_scale, HEAD_DIM).astype(BF16)
    ka_ref[0] = rot(p[:, A_Q:A_Q + A_KV], HEAD_DIM).astype(BF16)
    va_ref[0] = p[:, A_Q + A_KV:A_Q + 2 * A_KV].astype(BF16)

    p = _dot(h, w_ref[:, OFF_QB:OFF_KB])
    for j in range(B_QK // LANES):
        qb_ref[0, :, j * LANES:(j + 1) * LANES] = rot(
            p[:, j * LANES:(j + 1) * LANES] * b_scale, HEAD_DIM).astype(BF16)
    p = _dot(h, w_ref[:, OFF_KB:OFF_VB])
    for j in range(B_QK // LANES):
        kb_ref[0, :, j * LANES:(j + 1) * LANES] = rot(
            p[:, j * LANES:(j + 1) * LANES], HEAD_DIM).astype(BF16)
    vb_ref[0] = _dot(h, w_ref[:, OFF_VB:OFF_QR]).astype(BF16)

    p = _dot(h, w_ref[:, OFF_QR:OFF_VR])
    qr_ref[0] = rot(p[:, :C_QK], C_QK_DIM).astype(BF16)
    kr_ref[0] = rot(p[:, C_QK:] * c_scale, C_QK_DIM).astype(BF16)
    vr_ref[0] = _dot(h, w_ref[:, OFF_VR:OFF_GR]).astype(BF16)
    gr_ref[0] = _dot(h, w_ref[:, OFF_GR:IN_WIDTH])


def _in_proj(x, g, sh, sc, w, tabs, tm):
    bsz, n, d = x.shape
    rope = tabs is not None
    per_b = (lambda b, i: (b, 0, 0)) if sh.shape[0] == bsz else (lambda b, i: (0, 0, 0))
    in_specs = [
        pl.BlockSpec((1, tm, d), lambda b, i: (b, i, 0)),
        pl.BlockSpec((1, d), lambda b, i: (0, 0)),
        pl.BlockSpec((1, 1, d), per_b),
        pl.BlockSpec((1, 1, d), per_b),
        pl.BlockSpec((d, IN_WIDTH), lambda b, i: (0, 0)),
    ]
    args = [x, g.reshape(1, d), sh, sc, w]
    if rope:
        in_specs += [pl.BlockSpec((tm, LANES), lambda b, i: (i, 0))] * 4
        args += list(tabs)
    widths = (A_Q, A_KV, A_KV, B_QK, B_QK, B_V, C_QK, C_QK, C_V, C_V)
    dtypes = (BF16,) * 9 + (F32,)
    out_specs = [pl.BlockSpec((1, tm, wd), lambda b, i: (b, i, 0)) for wd in widths]
    out_shape = [jax.ShapeDtypeStruct((bsz, n, wd), dt) for wd, dt in zip(widths, dtypes)]
    return pl.pallas_call(
        functools.partial(_inproj_kernel, rope=rope),
        grid=(bsz, n // tm),
        in_specs=in_specs,
        out_specs=out_specs,
        out_shape=out_shape,
        compiler_params=_params("parallel", "parallel"),
        name="in_proj_rope" if rope else "in_proj_ctx",
    )(*args)


def _mixer_a_kernel(*refs, tq, n_lat, local):
    if local:
        sink_ref, q_ref, k_ref, v_ref, kc_ref, vc_ref, o_ref = refs
    else:
        sink_ref, q_ref, kc_ref, vc_ref, o_ref = refs
    blk = A_BLOCK
    half = HEAD_DIM
    lane = lax.broadcasted_iota(jnp.int32, (blk, LANES), 1)
    low = lane < half
    row2 = lax.broadcasted_iota(jnp.int32, (2 * blk, 1), 0)
    kc = kc_ref[0]
    vc = vc_ref[0]
    for sub in range(tq // blk):
        r0 = sub * blk
        gb = pl.program_id(1) * (tq // blk) + sub
        if local:
            start = pl.multiple_of(jnp.clip(gb * blk - blk, 0, n_lat - 3 * blk), blk)
            k_loc = k_ref[0, pl.ds(start, 3 * blk), :]
            v_loc = v_ref[0, pl.ds(start, 3 * blk), :]
            kpos = start + lax.broadcasted_iota(jnp.int32, (2 * blk, 3 * blk), 1)
            qpos = gb * blk + (lax.broadcasted_iota(jnp.int32, (2 * blk, 3 * blk), 0) & (blk - 1))
            valid = jnp.abs(kpos - qpos) <= A_WINDOW
        for hk in range(A_KV_HEADS):
            qp = q_ref[0, r0:r0 + blk, hk * LANES:(hk + 1) * LANES].astype(F32)
            qsw = pltpu.roll(qp, half, 1)
            if hk == 0:
                q0, q1 = jnp.where(low, qp, 0.0), jnp.where(low, qsw, 0.0)
            else:
                q0, q1 = jnp.where(low, 0.0, qsw), jnp.where(low, 0.0, qp)
            qz = jnp.concatenate([q0, q1], axis=0).astype(BF16)
            sink = jnp.where(row2 < blk, sink_ref[2 * hk], sink_ref[2 * hk + 1])
            s_ctx = _nt_dot(qz, kc)
            m = jnp.maximum(jnp.max(s_ctx, axis=-1, keepdims=True), sink)
            if local:
                s_loc = jnp.where(valid, _nt_dot(qz, k_loc), NEG_INF)
                m = jnp.maximum(m, jnp.max(s_loc, axis=-1, keepdims=True))
            e_ctx = jnp.exp(s_ctx - m)
            den = jnp.sum(e_ctx, axis=-1, keepdims=True) + jnp.exp(sink - m)
            r = _dot(e_ctx.astype(BF16), vc)
            if local:
                e_loc = jnp.exp(s_loc - m)
                den = den + jnp.sum(e_loc, axis=-1, keepdims=True)
                r = r + _dot(e_loc.astype(BF16), v_loc)
            r = r / den
            ra, rb = r[:blk], r[blk:]
            if hk == 0:
                pair = jnp.where(low, ra, pltpu.roll(rb, half, 1))
            else:
                pair = jnp.where(low, pltpu.roll(ra, half, 1), rb)
            o_ref[0, r0:r0 + blk, hk * LANES:(hk + 1) * LANES] = pair.astype(BF16)


def _mixer_a(sink, q, k, v, kc, vc, tq):
    bsz, n, _ = q.shape
    ctx_len = kc.shape[1]
    local = k is not None
    in_specs = [pl.BlockSpec(memory_space=pltpu.SMEM),
                pl.BlockSpec((1, tq, A_Q), lambda b, i: (b, i, 0))]
    args = [sink, q]
    if local:
        in_specs += [pl.BlockSpec((1, n, A_KV), lambda b, i: (b, 0, 0))] * 2
        args += [k, v]
    in_specs += [pl.BlockSpec((1, ctx_len, A_KV), lambda b, i: (b, 0, 0))] * 2
    args += [kc, vc]
    return pl.pallas_call(
        functools.partial(_mixer_a_kernel, tq=tq, n_lat=n, local=local),
        grid=(bsz, n // tq),
        in_specs=in_specs,
        out_specs=pl.BlockSpec((1, tq, A_Q), lambda b, i: (b, i, 0)),
        out_shape=jax.ShapeDtypeStruct((bsz, n, A_Q), BF16),
        compiler_params=_params("parallel", "parallel"),
        name="mixer_a_window" if local else "mixer_a_ctx",
    )(*args)


def _mixer_b_kernel(*refs, tq, tk, n_lat, lam_init):
    if n_lat:
        lam_ref, g_ref, q_ref, kc_ref, vc_ref, k_ref, v_ref, o_ref, acc_ref = refs
    else:
        lam_ref, g_ref, q_ref, kc_ref, vc_ref, o_ref, acc_ref = refs
    q = q_ref[0]
    lane = lax.broadcasted_iota(jnp.int32, q.shape, 1)
    zero = jnp.zeros_like(q)
    qz = jnp.concatenate([jnp.where(lane < B_QK_DIM, q, zero),
                          jnp.where(lane < B_QK_DIM, zero, q)], axis=0)

    def step(kblk, vblk, m, l, first):
        s = _nt_dot(qz, kblk)
        m_new = jnp.maximum(m, jnp.max(s, axis=-1, keepdims=True))
        alpha = jnp.exp(m - m_new)
        p = jnp.exp(s - m_new)
        l_new = alpha * l + jnp.sum(p, axis=-1, keepdims=True)
        pv = _dot(p.astype(BF16), vblk)
        if first:
            acc_ref[...] = pv
        else:
            acc_ref[...] = alpha * acc_ref[...] + pv
        return m_new, l_new

    m0 = jnp.full((2 * tq, 1), NEG_INF, F32)
    l0 = jnp.zeros((2 * tq, 1), F32)
    m, l = step(kc_ref[0], vc_ref[0], m0, l0, True)
    if n_lat:
        def body(j, carry):
            off = pl.multiple_of(j * tk, tk)
            return step(k_ref[0, pl.ds(off, tk), :], v_ref[0, pl.ds(off, tk), :],
                        carry[0], carry[1], False)
        m, l = lax.fori_loop(0, n_lat // tk, body, (m, l))

    d = lam_ref[...]
    lam = (jnp.exp(jnp.sum(d[0:1] * d[1:2], axis=-1, keepdims=True))
           - jnp.exp(jnp.sum(d[2:3] * d[3:4], axis=-1, keepdims=True)) + lam_init)
    acc = acc_ref[...] / l
    o = acc[:tq] - lam * acc[tq:]
    y = _rms(o) * g_ref[...] * (1.0 - lam_init)
    o_ref[0] = y.astype(BF16)


def _mixer_b(lam_p, g, q, kc, vc, k, v, lam_init, tq, tk):
    bsz, n, _ = q.shape
    ctx_len = kc.shape[1]
    has_lat = k is not None
    n_lat = k.shape[1] if has_lat else 0
    in_specs = [
        pl.BlockSpec((4, B_QK_DIM), lambda b, h, i: (0, 0)),
        pl.BlockSpec((1, B_V_DIM), lambda b, h, i: (0, 0)),
        pl.BlockSpec((1, tq, LANES), lambda b, h, i: (b, i, h)),
        pl.BlockSpec((1, ctx_len, LANES), lambda b, h, i: (b, 0, h)),
        pl.BlockSpec((1, ctx_len, LANES), lambda b, h, i: (b, 0, h)),
    ]
    args = [lam_p, g.reshape(1, B_V_DIM), q, kc, vc]
    if has_lat:
        in_specs += [pl.BlockSpec((1, n_lat, LANES), lambda b, h, i: (b, 0, h))] * 2
        args += [k, v]
    return pl.pallas_call(
        functools.partial(_mixer_b_kernel, tq=tq, tk=tk, n_lat=n_lat, lam_init=lam_init),
        grid=(bsz, B_HEADS, n // tq),
        in_specs=in_specs,
        out_specs=pl.BlockSpec((1, tq, LANES), lambda b, h, i: (b, i, h)),
        out_shape=jax.ShapeDtypeStruct((bsz, n, B_V), BF16),
        scratch_shapes=[pltpu.VMEM((2 * tq, B_V_DIM), F32)],
        compiler_params=_params("parallel", "parallel", "parallel"),
        name="mixer_b_latent" if has_lat else "mixer_b_ctx",
    )(*args)


def _log_sigmoid(x):
    return jnp.minimum(x, 0.0) - jnp.log(1.0 + jnp.exp(-jnp.abs(x)))


def _per_head(ref, row, shape, axis, width):
    idx = lax.broadcasted_iota(jnp.int32, shape, axis) // width
    out = jnp.zeros(shape, F32)
    for h in range(C_HEADS):
        out = jnp.where(idx == h, ref[row, h], out)
    return out


def _mixer_c_kernel(*refs, n, ctx_len):
    if ctx_len:
        dec_ref, q_ref, k_ref, v_ref, g_ref, kc_ref, vc_ref, o_ref, sf_ref = refs
    else:
        dec_ref, q_ref, k_ref, v_ref, g_ref, o_ref, sf_ref = refs
    ch = C_CHUNK
    nc = n // ch
    qk_w, v_w = C_QK, C_V

    lgf_qk = _log_sigmoid(_per_head(dec_ref, 0, (1, qk_w), 1, C_QK_DIM))
    lgb_qk = _log_sigmoid(_per_head(dec_ref, 1, (1, qk_w), 1, C_QK_DIM))
    lgf_blk = _log_sigmoid(_per_head(dec_ref, 0, (1, C_HEADS * ch), 1, ch))
    lgb_blk = _log_sigmoid(_per_head(dec_ref, 1, (1, C_HEADS * ch), 1, ch))
    lgf_row = _log_sigmoid(_per_head(dec_ref, 0, (qk_w, 1), 0, C_QK_DIM))
    lgb_row = _log_sigmoid(_per_head(dec_ref, 1, (qk_w, 1), 0, C_QK_DIM))

    pos = lax.broadcasted_iota(jnp.int32, (ch, 1), 0).astype(F32)
    q_dec_f = jnp.exp(lgf_qk * (pos + 1.0))
    q_dec_b = jnp.exp(lgb_qk * (ch - pos))
    k_dec_f = jnp.exp(lgf_qk * (ch - 1.0 - pos))
    k_dec_b = jnp.exp(lgb_qk * pos)
    chunk_f = jnp.exp(lgf_row * float(ch))
    chunk_b = jnp.exp(lgb_row * float(ch))
    col = lax.broadcasted_iota(jnp.int32, (ch, C_HEADS * ch), 1) & (ch - 1)
    diff = (lax.broadcasted_iota(jnp.int32, (ch, C_HEADS * ch), 0) - col).astype(F32)
    decay = jnp.where(diff >= 0.0, jnp.exp(lgf_blk * jnp.maximum(diff, 0.0)),
                      jnp.exp(lgb_blk * jnp.maximum(-diff, 0.0)))
    state_mask = (lax.broadcasted_iota(jnp.int32, (qk_w, v_w), 0) // C_QK_DIM
                  == lax.broadcasted_iota(jnp.int32, (qk_w, v_w), 1) // C_V_DIM)
    head_qk = lax.broadcasted_iota(jnp.int32, (ch, qk_w), 1) // C_QK_DIM
    head_v = lax.broadcasted_iota(jnp.int32, (ch, v_w), 1) // C_V_DIM

    def kv_update(kf32, vb16, weights):
        kk = jnp.concatenate([kf32 * w for w in weights], axis=1) if len(weights) > 1 \
            else kf32 * weights[0]
        u = _dot(kk.T.astype(BF16), vb16)
        return [jnp.where(state_mask, u[j * qk_w:(j + 1) * qk_w], 0.0) for j in range(len(weights))]

    if ctx_len:
        cpos = lax.broadcasted_iota(jnp.int32, (ctx_len, 1), 0).astype(F32)
        s_f, s_b = kv_update(kc_ref[0].astype(F32), vc_ref[0],
                             [jnp.exp(lgf_qk * (ctx_len - 1.0 - cpos)), jnp.exp(lgb_qk * cpos)])
    else:
        s_f = jnp.zeros((qk_w, v_w), F32)
        s_b = jnp.zeros((qk_w, v_w), F32)

    def fwd(c, state):
        off = pl.multiple_of(c * ch, ch)
        sf_ref[c] = state.astype(BF16)
        (u_f,) = kv_update(k_ref[0, pl.ds(off, ch), :].astype(F32), v_ref[0, pl.ds(off, ch), :],
                           [k_dec_f])
        return chunk_f * state + u_f

    lax.fori_loop(0, nc, fwd, s_f)

    def bwd(t, state_b):
        c = nc - 1 - t
        off = pl.multiple_of(c * ch, ch)
        q = q_ref[0, pl.ds(off, ch), :]
        k = k_ref[0, pl.ds(off, ch), :]
        v = v_ref[0, pl.ds(off, ch), :]
        zq = jnp.zeros_like(k)
        zv = jnp.zeros_like(v)
        kz = jnp.concatenate([jnp.where(head_qk == h, k, zq) for h in range(C_HEADS)], axis=0)
        vz = jnp.concatenate([jnp.where(head_v == h, v, zv) for h in range(C_HEADS)], axis=0)
        att = (_nt_dot(q, kz) * decay).astype(BF16)
        y = _dot(att, vz)
        qf = q.astype(F32)
        qq = jnp.concatenate([qf * q_dec_f, qf * q_dec_b], axis=1).astype(BF16)
        states = jnp.concatenate([sf_ref[c], state_b.astype(BF16)], axis=0)
        y = y + _dot(qq, states)
        outs = []
        for j in range(v_w // LANES):
            yb = y[:, j * LANES:(j + 1) * LANES]
            lo = lax.broadcasted_iota(jnp.int32, yb.shape, 1) < C_V_DIM
            s_all = jnp.sum(yb, axis=-1, keepdims=True)
            s_lo = jnp.sum(jnp.where(lo, yb, 0.0), axis=-1, keepdims=True)
            yc = yb - jnp.where(lo, s_lo, s_all - s_lo) * (1.0 / C_V_DIM)
            sq = yc * yc
            q_all = jnp.sum(sq, axis=-1, keepdims=True)
            q_lo = jnp.sum(jnp.where(lo, sq, 0.0), axis=-1, keepdims=True)
            var = jnp.where(lo, q_lo, q_all - q_lo) * (1.0 / C_V_DIM)
            outs.append(yc * lax.rsqrt(var + NORM_EPS))
        yn = jnp.concatenate(outs, axis=1)
        gate = g_ref[0, pl.ds(off, ch), :]
        o_ref[0, pl.ds(off, ch), :] = (gate * _sigmoid(gate) * yn).astype(BF16)
        (u_b,) = kv_update(k.astype(F32), v, [k_dec_b])
        return chunk_b * state_b + u_b

    lax.fori_loop(0, nc, bwd, s_b)


def _mixer_c(dec, q, k, v, g, kc, vc):
    bsz, n, _ = q.shape
    ctx_len = kc.shape[1] if kc is not None else 0
    seq = lambda wd: pl.BlockSpec((1, n, wd), lambda b: (b, 0, 0))
    once = lambda wd: pl.BlockSpec((1, n, wd), lambda b: (b, 0, 0), pipeline_mode=pl.Buffered(1))
    in_specs = [pl.BlockSpec(memory_space=pltpu.SMEM), once(C_QK), once(C_QK), once(C_V), once(C_V)]
    args = [dec, q, k, v, g]
    if ctx_len:
        in_specs += [pl.BlockSpec((1, ctx_len, C_QK), lambda b: (b, 0, 0)),
                     pl.BlockSpec((1, ctx_len, C_V), lambda b: (b, 0, 0))]
        args += [kc, vc]
    return pl.pallas_call(
        functools.partial(_mixer_c_kernel, n=n, ctx_len=ctx_len),
        grid=(bsz,),
        in_specs=in_specs,
        out_specs=seq(C_V),
        out_shape=jax.ShapeDtypeStruct((bsz, n, C_V), BF16),
        scratch_shapes=[pltpu.VMEM((n // C_CHUNK, C_QK, C_V), BF16)],
        compiler_params=_params("parallel"),
        name="mixer_c_latent" if ctx_len else "mixer_c_ctx",
    )(*args)


def _outproj_kernel(a_ref, b_ref, c_ref, x_ref, w_ref, g1_ref, n2_ref, sh_ref, sc_ref,
                    xo_ref, h_ref):
    y = (_dot(a_ref[0], w_ref[0:A_Q, :])
         + _dot(b_ref[0], w_ref[A_Q:A_Q + B_V, :])
         + _dot(c_ref[0], w_ref[A_Q + B_V:, :]))
    xn = x_ref[0] + g1_ref[0] * y
    xo_ref[0] = xn
    h = (_rms(xn) * n2_ref[...]) * (1.0 + sc_ref[0]) + sh_ref[0]
    h_ref[0] = h.astype(BF16)


def _out_proj(a, b, c, x, w, g1, n2, sh, sc, tm):
    bsz, n, d = x.shape
    per_b = (lambda bb, i: (bb, 0, 0)) if g1.shape[0] == bsz else (lambda bb, i: (0, 0, 0))
    tile = lambda wd: pl.BlockSpec((1, tm, wd), lambda bb, i: (bb, i, 0))
    vec = pl.BlockSpec((1, 1, d), per_b)
    return pl.pallas_call(
        _outproj_kernel,
        grid=(bsz, n // tm),
        in_specs=[tile(A_Q), tile(B_V), tile(C_V), tile(d),
                  pl.BlockSpec((d, d), lambda bb, i: (0, 0)),
                  vec, pl.BlockSpec((1, d), lambda bb, i: (0, 0)), vec, vec],
        out_specs=[tile(d), tile(d)],
        out_shape=[jax.ShapeDtypeStruct((bsz, n, d), F32), jax.ShapeDtypeStruct((bsz, n, d), BF16)],
        compiler_params=_params("parallel", "parallel"),
        name="out_proj",
    )(a, b, c, x, w, g1, n2.reshape(1, d), sh, sc)


FF_CHUNK = 256


def _ffn_kernel(x_ref, hm_ref, hp_ref, hn_ref, wup_ref, cw_ref, cb_ref, wdn_ref, g2_ref, fg_ref,
                o_ref, hs_ref, act_ref, *, tm, d_ff, final):
    i = pl.program_id(1)
    last = pl.num_programs(1) - 1
    halo = SUBLANES * 2
    prev = hp_ref[0]
    nxt = hn_ref[0]
    hs_ref[0:halo, :] = jnp.where(i > 0, prev, jnp.zeros_like(prev))
    hs_ref[halo:halo + tm, :] = hm_ref[0]
    hs_ref[halo + tm:2 * halo + tm, :] = jnp.where(i < last, nxt, jnp.zeros_like(nxt))
    hs = hs_ref[...]

    def conv(u, off):
        w = cw_ref[:, off:off + FF_CHUNK]
        return (u[halo - 1:halo - 1 + tm] * w[0:1] + u[halo:halo + tm] * w[1:2]
                + u[halo + 1:halo + 1 + tm] * w[2:3] + cb_ref[:, off:off + FF_CHUNK])

    for c in range(d_ff // FF_CHUNK):
        lo = c * FF_CHUNK
        val = conv(_dot(hs, wup_ref[:, lo:lo + FF_CHUNK]), lo)
        gate = conv(_dot(hs, wup_ref[:, d_ff + lo:d_ff + lo + FF_CHUNK]), d_ff + lo)
        act_ref[:, lo:lo + FF_CHUNK] = (gate * _sigmoid(gate) * val).astype(BF16)
    y = _dot(act_ref[...], wdn_ref[...])
    out = x_ref[0] + g2_ref[0] * y
    if final:
        out = _rms(out) * fg_ref[...]
    o_ref[0] = out


def _ffn(x, h, w_up, conv_w, conv_b, w_down, g2, final_g, tm, final):
    bsz, n, d = x.shape
    d_ff = w_down.shape[0]
    halo = SUBLANES * 2
    nb = tm // halo
    n_halo_blocks = n // halo
    per_b = (lambda bb, i: (bb, 0, 0)) if g2.shape[0] == bsz else (lambda bb, i: (0, 0, 0))
    const = lambda shape: pl.BlockSpec(shape, lambda bb, i: (0, 0), pipeline_mode=pl.Buffered(1))
    return pl.pallas_call(
        functools.partial(_ffn_kernel, tm=tm, d_ff=d_ff, final=final),
        grid=(bsz, n // tm),
        in_specs=[
            pl.BlockSpec((1, tm, d), lambda bb, i: (bb, i, 0)),
            pl.BlockSpec((1, tm, d), lambda bb, i: (bb, i, 0)),
            pl.BlockSpec((1, halo, d), lambda bb, i: (bb, jnp.maximum(i * nb - 1, 0), 0)),
            pl.BlockSpec((1, halo, d),
                         lambda bb, i: (bb, jnp.minimum((i + 1) * nb, n_halo_blocks - 1), 0)),
            const((d, 2 * d_ff)),
            const((CONV_W, 2 * d_ff)),
            const((1, 2 * d_ff)),
            const((d_ff, d)),
            pl.BlockSpec((1, 1, d), per_b),
            const((1, d)),
        ],
        out_specs=pl.BlockSpec((1, tm, d), lambda bb, i: (bb, i, 0)),
        out_shape=jax.ShapeDtypeStruct((bsz, n, d), F32),
        scratch_shapes=[pltpu.VMEM((tm + 2 * halo, d), BF16), pltpu.VMEM((tm, d_ff), BF16)],
        compiler_params=_params("parallel", "parallel"),
        name="ffn_final" if final else "ffn",
    )(x, h, h, h, w_up, conv_w, conv_b.reshape(1, 2 * d_ff), w_down, g2, final_g.reshape(1, d))


def _rope_tables(n_tok, dim):
    n_rows = n_tok // GRID_W
    rows = jnp.repeat(jnp.arange(n_rows, dtype=F32), GRID_W)
    cols = jnp.tile(jnp.arange(GRID_W, dtype=F32), n_rows)
    n_freq = dim // 4
    inv = ROPE_BASE ** (-jnp.arange(n_freq, dtype=F32) / n_freq)
    ang = jnp.concatenate([rows[:, None] * inv, cols[:, None] * inv], axis=-1)
    cos, sin = jnp.cos(ang), jnp.sin(ang)
    reps = LANES // dim
    cos_t = jnp.tile(jnp.concatenate([cos, cos], axis=-1), (1, reps))
    sin_t = jnp.tile(jnp.concatenate([-sin, sin], axis=-1), (1, reps))
    return cos_t, sin_t


def _tile_rows(n, target):
    t = min(n, target)
    while n % t:
        t //= 2
    return t


def kernel(x, c, ctx, c_ctx, w_mod, b_mod, norm1_g, norm2_g, w_in, w_out, attn_sink,
           diff_lambda, diff_subln_g, ret_decay_logit, w_up, conv_w, conv_b, w_down, final_g):
    bsz, n_lat, d = x.shape
    depth = w_mod.shape[0]
    tabs = _rope_tables(n_lat, HEAD_DIM) + _rope_tables(n_lat, C_QK_DIM)

    rows = -(-(bsz + 1) // SUBLANES) * SUBLANES
    cc = jnp.concatenate([c, c_ctx[None, :], jnp.zeros((rows - bsz - 1, d), F32)], axis=0)
    mod = _modulation(cc, w_mod, b_mod)

    w_in_b, w_out_b = w_in.astype(BF16), w_out.astype(BF16)
    w_up_b, w_down_b = w_up.astype(BF16), w_down.astype(BF16)

    tm_lat = _tile_rows(n_lat, 512)
    tm_ctx = _tile_rows(ctx.shape[1], 512)
    xc = ctx
    for l in range(depth):
        is_last = l == depth - 1
        lam_init = 0.8 - 0.6 * math.exp(-0.3 * l)
        m_lat = [mod[l, :bsz, j * d:(j + 1) * d][:, None, :] for j in range(N_MOD)]
        m_ctx = [mod[l, bsz:bsz + 1, j * d:(j + 1) * d][:, None, :] for j in range(N_MOD)]
        sh1, sc1, g1, sh2, sc2, g2 = m_lat
        csh1, csc1, cg1, csh2, csc2, cg2 = m_ctx

        (qa, ka, va, qb, kb, vb, qr, kr, vr, gr) = _in_proj(
            x, norm1_g[l], sh1, sc1, w_in_b[l], tabs, tm_lat)
        (qa_c, ka_c, va_c, qb_c, kb_c, vb_c, qr_c, kr_c, vr_c, gr_c) = _in_proj(
            xc, norm1_g[l], csh1, csc1, w_in_b[l], None, tm_ctx)

        a_out = _mixer_a(attn_sink[l], qa, ka, va, ka_c, va_c, tm_lat)
        b_out = _mixer_b(diff_lambda[l], diff_subln_g[l], qb, kb_c, vb_c, kb, vb,
                         lam_init, _tile_rows(n_lat, 256), _tile_rows(n_lat, 512))
        c_out = _mixer_c(ret_decay_logit[l], qr, kr, vr, gr, kr_c, vr_c)
        x_mid, h2 = _out_proj(a_out, b_out, c_out, x, w_out_b[l], g1, norm2_g[l], sh2, sc2, tm_lat)
        x = _ffn(x_mid, h2, w_up_b[l], conv_w[l], conv_b[l], w_down_b[l], g2, final_g,
                 tm_lat, is_last)

        if not is_last:
            a_c = _mixer_a(attn_sink[l], qa_c, None, None, ka_c, va_c, tm_ctx)
            b_c = _mixer_b(diff_lambda[l], diff_subln_g[l], qb_c, kb_c, vb_c, None, None,
                           lam_init, tm_ctx, tm_ctx)
            c_c = _mixer_c(ret_decay_logit[l], qr_c, kr_c, vr_c, gr_c, None, None)
            xc_mid, hc2 = _out_proj(a_c, b_c, c_c, xc, w_out_b[l], cg1, norm2_g[l], csh2, csc2,
                                    tm_ctx)
            xc = _ffn(xc_mid, hc2, w_up_b[l], conv_w[l], conv_b[l], w_down_b[l], cg2, final_g,
                      tm_ctx, False)
    return x
```

```python
import functools
import math

import jax
import jax.numpy as jnp
from jax import lax
from jax.experimental import pallas as pl
from jax.experimental.pallas import tpu as pltpu

F32 = jnp.float32
BF16 = jnp.bfloat16

GRID_W = 64
HEAD_DIM = 64
ROPE_BASE = 10000.0
NORM_EPS = 1e-6
NEG_INF = -1e30

A_HEADS = 4
A_KV_HEADS = 2
A_WINDOW = 128
A_BLOCK = 128
B_HEADS = 4
B_QK_DIM = 64
B_V_DIM = 128
C_HEADS = 4
C_QK_DIM = 32
C_V_DIM = 64
C_CHUNK = 128

A_Q = A_HEADS * HEAD_DIM
A_KV = A_KV_HEADS * HEAD_DIM
B_QK = B_HEADS * 2 * B_QK_DIM
B_V = B_HEADS * B_V_DIM
C_QK = C_HEADS * C_QK_DIM
C_V = C_HEADS * C_V_DIM
IN_WIDTH = A_Q + 2 * A_KV + 2 * B_QK + B_V + 2 * C_QK + 2 * C_V
N_MOD = 6
CONV_W = 3

LANES = 128
SUBLANES = 8
VMEM_LIMIT = 56 * 1024 * 1024

OFF_QA = 0
OFF_QB = A_Q + 2 * A_KV
OFF_KB = OFF_QB + B_QK
OFF_VB = OFF_KB + B_QK
OFF_QR = OFF_VB + B_V
OFF_VR = OFF_QR + 2 * C_QK
OFF_GR = OFF_VR + C_V


def _params(*sem):
    return pltpu.CompilerParams(dimension_semantics=sem, vmem_limit_bytes=VMEM_LIMIT)


def _nt_dot(a, b):
    return lax.dot_general(a, b, (((1,), (1,)), ((), ())), preferred_element_type=F32)


def _dot(a, b):
    return jnp.dot(a, b, preferred_element_type=F32)


def _rms(x):
    return x * lax.rsqrt(jnp.mean(x * x, axis=-1, keepdims=True) + NORM_EPS)


def _sigmoid(x):
    return 1.0 / (1.0 + jnp.exp(-x))


def _mod_kernel(c_ref, w_ref, b_ref, o_ref):
    c = c_ref[...]
    a = (c * _sigmoid(c)).astype(BF16)
    o_ref[0] = _dot(a, w_ref[0].astype(BF16)) + b_ref[0]


def _modulation(cc, w_mod, b_mod):
    depth, d, width = w_mod.shape
    rows = cc.shape[0]
    blk = d
    return pl.pallas_call(
        _mod_kernel,
        grid=(depth, width // blk),
        in_specs=[
            pl.BlockSpec((rows, d), lambda l, j: (0, 0)),
            pl.BlockSpec((1, d, blk), lambda l, j: (l, 0, j)),
            pl.BlockSpec((1, 1, blk), lambda l, j: (l, 0, j)),
        ],
        out_specs=pl.BlockSpec((1, rows, blk), lambda l, j: (l, 0, j)),
        out_shape=jax.ShapeDtypeStruct((depth, rows, width), F32),
        compiler_params=_params("arbitrary", "arbitrary"),
        name="modulation",
    )(cc, w_mod, b_mod.reshape(depth, 1, width))


def _rope_block(blk, cos, sin_signed, half):
    lane = lax.broadcasted_iota(jnp.int32, blk.shape, 1)
    first = (lane & (2 * half - 1)) < half
    swapped = jnp.where(first, pltpu.roll(blk, LANES - half, 1), pltpu.roll(blk, half, 1))
    return blk * cos + swapped * sin_signed


def _inproj_kernel(*refs, rope):
    if rope:
        (x_ref, g_ref, sh_ref, sc_ref, w_ref, c64_ref, s64_ref, c32_ref, s32_ref,
         qa_ref, ka_ref, va_ref, qb_ref, kb_ref, vb_ref, qr_ref, kr_ref, vr_ref, gr_ref) = refs
    else:
        (x_ref, g_ref, sh_ref, sc_ref, w_ref,
         qa_ref, ka_ref, va_ref, qb_ref, kb_ref, vb_ref, qr_ref, kr_ref, vr_ref, gr_ref) = refs
    x = x_ref[0]
    h = (_rms(x) * g_ref[...]) * (1.0 + sc_ref[0]) + sh_ref[0]
    h = h.astype(BF16)

    def rot(blk, dim):
        if not rope:
            return blk
        if dim == HEAD_DIM:
            return _rope_block(blk, c64_ref[...], s64_ref[...], HEAD_DIM // 2)
        return _rope_block(blk, c32_ref[...], s32_ref[...], C_QK_DIM // 2)

    a_scale = HEAD_DIM ** -0.5
    b_scale = B_QK_DIM ** -0.5 * math.log2(math.e)
    c_scale = C_QK_DIM ** -0.5

    p = _dot(h, w_ref[:, OFF_QA:OFF_QB])
    for j in range(A_Q // LANES):
        qa_ref[0, :, j * LANES:(j + 1) * LANES] = rot(
            p[:, j * LANES:(j + 1) * LANES] * a_scale, HEAD_DIM).astype(BF16)
    ka_ref[0] = rot(p[:, A_Q:A_Q + A_KV], HEAD_DIM).astype(BF16)
    va_ref[0] = p[:, A_Q + A_KV:A_Q + 2 * A_KV].astype(BF16)

    p = _dot(h, w_ref[:, OFF_QB:OFF_KB])
    for j in range(B_QK // LANES):
        qb_ref[0, :, j * LANES:(j + 1) * LANES] = rot(
            p[:, j * LANES:(j + 1) * LANES] * b_scale, HEAD_DIM).astype(BF16)
    p = _dot(h, w_ref[:, OFF_KB:OFF_VB])
    for j in range(B_QK // LANES):
        kb_ref[0, :, j * LANES:(j + 1) * LANES] = rot(
            p[:, j * LANES:(j + 1) * LANES], HEAD_DIM).astype(BF16)
    p = _dot(h, w_ref[:, OFF_VB:OFF_QR]).astype(BF16)
    for j in range(B_HEADS):
        vb_ref[0, :, 2 * j * B_V_DIM:(2 * j + 1) * B_V_DIM] = p[:, j * B_V_DIM:(j + 1) * B_V_DIM]
        vb_ref[0, :, (2 * j + 1) * B_V_DIM:(2 * j + 2) * B_V_DIM] = jnp.ones(
            (p.shape[0], B_V_DIM), BF16)

    p = _dot(h, w_ref[:, OFF_QR:OFF_VR])
    qr_ref[0] = rot(p[:, :C_QK], C_QK_DIM).astype(BF16)
    kr_ref[0] = rot(p[:, C_QK:] * c_scale, C_QK_DIM).astype(BF16)
    vr_ref[0] = _dot(h, w_ref[:, OFF_VR:OFF_GR]).astype(BF16)
    gr_ref[0] = _dot(h, w_ref[:, OFF_GR:IN_WIDTH])


def _in_proj(x, g, sh, sc, w, tabs, tm):
    bsz, n, d = x.shape
    rope = tabs is not None
    per_b = (lambda b, i: (b, 0, 0)) if sh.shape[0] == bsz else (lambda b, i: (0, 0, 0))
    in_specs = [
        pl.BlockSpec((1, tm, d), lambda b, i: (b, i, 0)),
        pl.BlockSpec((1, d), lambda b, i: (0, 0)),
        pl.BlockSpec((1, 1, d), per_b),
        pl.BlockSpec((1, 1, d), per_b),
        pl.BlockSpec((d, IN_WIDTH), lambda b, i: (0, 0)),
    ]
    args = [x, g.reshape(1, d), sh, sc, w]
    if rope:
        in_specs += [pl.BlockSpec((tm, LANES), lambda b, i: (i, 0))] * 4
        args += list(tabs)
    widths = (A_Q, A_KV, A_KV, B_QK, B_QK, 2 * B_V, C_QK, C_QK, C_V, C_V)
    dtypes = (BF16,) * 9 + (F32,)
    out_specs = [pl.BlockSpec((1, tm, wd), lambda b, i: (b, i, 0)) for wd in widths]
    out_shape = [jax.ShapeDtypeStruct((bsz, n, wd), dt) for wd, dt in zip(widths, dtypes)]
    return pl.pallas_call(
        functools.partial(_inproj_kernel, rope=rope),
        grid=(bsz, n // tm),
        in_specs=in_specs,
        out_specs=out_specs,
        out_shape=out_shape,
        compiler_params=_params("parallel", "parallel"),
        name="in_proj_rope" if rope else "in_proj_ctx",
    )(*args)


def _mixer_a_kernel(*refs, tq, n_lat, local):
    if local:
        sink_ref, q_ref, k_ref, v_ref, kc_ref, vc_ref, o_ref = refs
    else:
        sink_ref, q_ref, kc_ref, vc_ref, o_ref = refs
    blk = A_BLOCK
    half = HEAD_DIM
    lane = lax.broadcasted_iota(jnp.int32, (blk, LANES), 1)
    low = lane < half
    row2 = lax.broadcasted_iota(jnp.int32, (2 * blk, 1), 0)
    kc = kc_ref[0]
    vc = vc_ref[0]
    for sub in range(tq // blk):
        r0 = sub * blk
        gb = pl.program_id(1) * (tq // blk) + sub
        if local:
            start = pl.multiple_of(jnp.clip(gb * blk - blk, 0, n_lat - 3 * blk), blk)
            k_loc = k_ref[0, pl.ds(start, 3 * blk), :]
            v_loc = v_ref[0, pl.ds(start, 3 * blk), :]
            kpos = start + lax.broadcasted_iota(jnp.int32, (2 * blk, 3 * blk), 1)
            qpos = gb * blk + (lax.broadcasted_iota(jnp.int32, (2 * blk, 3 * blk), 0) & (blk - 1))
            valid = jnp.abs(kpos - qpos) <= A_WINDOW
        for hk in range(A_KV_HEADS):
            qp = q_ref[0, r0:r0 + blk, hk * LANES:(hk + 1) * LANES].astype(F32)
            qsw = pltpu.roll(qp, half, 1)
            if hk == 0:
                q0, q1 = jnp.where(low, qp, 0.0), jnp.where(low, qsw, 0.0)
            else:
                q0, q1 = jnp.where(low, 0.0, qsw), jnp.where(low, 0.0, qp)
            qz = jnp.concatenate([q0, q1], axis=0).astype(BF16)
            sink = jnp.where(row2 < blk, sink_ref[2 * hk], sink_ref[2 * hk + 1])
            s_ctx = _nt_dot(qz, kc)
            m = jnp.maximum(jnp.max(s_ctx, axis=-1, keepdims=True), sink)
            if local:
                s_loc = jnp.where(valid, _nt_dot(qz, k_loc), NEG_INF)
                m = jnp.maximum(m, jnp.max(s_loc, axis=-1, keepdims=True))
            e_ctx = jnp.exp(s_ctx - m)
            den = jnp.sum(e_ctx, axis=-1, keepdims=True) + jnp.exp(sink - m)
            r = _dot(e_ctx.astype(BF16), vc)
            if local:
                e_loc = jnp.exp(s_loc - m)
                den = den + jnp.sum(e_loc, axis=-1, keepdims=True)
                r = r + _dot(e_loc.astype(BF16), v_loc)
            r = r / den
            ra, rb = r[:blk], r[blk:]
            if hk == 0:
                pair = jnp.where(low, ra, pltpu.roll(rb, half, 1))
            else:
                pair = jnp.where(low, pltpu.roll(ra, half, 1), rb)
            o_ref[0, r0:r0 + blk, hk * LANES:(hk + 1) * LANES] = pair.astype(BF16)


def _mixer_a(sink, q, k, v, kc, vc, tq):
    bsz, n, _ = q.shape
    ctx_len = kc.shape[1]
    local = k is not None
    in_specs = [pl.BlockSpec(memory_space=pltpu.SMEM),
                pl.BlockSpec((1, tq, A_Q), lambda b, i: (b, i, 0))]
    args = [sink, q]
    if local:
        in_specs += [pl.BlockSpec((1, n, A_KV), lambda b, i: (b, 0, 0))] * 2
        args += [k, v]
    in_specs += [pl.BlockSpec((1, ctx_len, A_KV), lambda b, i: (b, 0, 0))] * 2
    args += [kc, vc]
    return pl.pallas_call(
        functools.partial(_mixer_a_kernel, tq=tq, n_lat=n, local=local),
        grid=(bsz, n // tq),
        in_specs=in_specs,
        out_specs=pl.BlockSpec((1, tq, A_Q), lambda b, i: (b, i, 0)),
        out_shape=jax.ShapeDtypeStruct((bsz, n, A_Q), BF16),
        compiler_params=_params("parallel", "parallel"),
        name="mixer_a_window" if local else "mixer_a_ctx",
    )(*args)


def _mixer_b_kernel(*refs, tq, tk, n_lat, lam_init):
    if n_lat:
        lam_ref, g_ref, q_ref, kc_ref, vc_ref, k_ref, v_ref, o_ref, acc_ref = refs
    else:
        lam_ref, g_ref, q_ref, kc_ref, vc_ref, o_ref, acc_ref = refs
    q = q_ref[0]
    lane = lax.broadcasted_iota(jnp.int32, q.shape, 1)
    zero = jnp.zeros_like(q)
    qz = (jnp.where(lane < B_QK_DIM, q, zero), jnp.where(lane < B_QK_DIM, zero, q))

    def step(kblk, vblk, ms, first):
        out = []
        for c in range(2):
            s = _nt_dot(qz[c], kblk)
            m_new = jnp.maximum(ms[c], jnp.max(s, axis=-1, keepdims=True))
            p = jnp.exp2(s - m_new)
            pv = _dot(p.astype(BF16), vblk)
            if first:
                acc_ref[c] = pv
            else:
                acc_ref[c] = jnp.exp2(ms[c] - m_new) * acc_ref[c] + pv
            out.append(m_new)
        return tuple(out)

    m0 = jnp.full((tq, 1), NEG_INF, F32)
    ms = step(kc_ref[0], vc_ref[0], (m0, m0), True)
    if n_lat:
        def body(j, ms):
            off = pl.multiple_of(j * tk, tk)
            return step(k_ref[0, pl.ds(off, tk), :], v_ref[0, pl.ds(off, tk), :], ms, False)
        ms = lax.fori_loop(0, n_lat // tk, body, ms, unroll=8)

    d = lam_ref[...]
    lam = (jnp.exp(jnp.sum(d[0:1] * d[1:2], axis=-1, keepdims=True))
           - jnp.exp(jnp.sum(d[2:3] * d[3:4], axis=-1, keepdims=True)) + lam_init)
    o = (acc_ref[0, :, :B_V_DIM] / acc_ref[0, :, B_V_DIM:]
         - lam * (acc_ref[1, :, :B_V_DIM] / acc_ref[1, :, B_V_DIM:]))
    y = _rms(o) * g_ref[...] * (1.0 - lam_init)
    o_ref[0] = y.astype(BF16)


def _mixer_b(lam_p, g, q, kc, vc, k, v, lam_init, tq, tk):
    bsz, n, _ = q.shape
    ctx_len = kc.shape[1]
    has_lat = k is not None
    n_lat = k.shape[1] if has_lat else 0
    in_specs = [
        pl.BlockSpec((4, B_QK_DIM), lambda b, h, i: (0, 0)),
        pl.BlockSpec((1, B_V_DIM), lambda b, h, i: (0, 0)),
        pl.BlockSpec((1, tq, LANES), lambda b, h, i: (b, i, h)),
        pl.BlockSpec((1, ctx_len, LANES), lambda b, h, i: (b, 0, h)),
        pl.BlockSpec((1, ctx_len, 2 * B_V_DIM), lambda b, h, i: (b, 0, h)),
    ]
    args = [lam_p, g.reshape(1, B_V_DIM), q, kc, vc]
    if has_lat:
        in_specs += [pl.BlockSpec((1, n_lat, LANES), lambda b, h, i: (b, 0, h)),
                     pl.BlockSpec((1, n_lat, 2 * B_V_DIM), lambda b, h, i: (b, 0, h))]
        args += [k, v]
    return pl.pallas_call(
        functools.partial(_mixer_b_kernel, tq=tq, tk=tk, n_lat=n_lat, lam_init=lam_init),
        grid=(bsz, B_HEADS, n // tq),
        in_specs=in_specs,
        out_specs=pl.BlockSpec((1, tq, LANES), lambda b, h, i: (b, i, h)),
        out_shape=jax.ShapeDtypeStruct((bsz, n, B_V), BF16),
        scratch_shapes=[pltpu.VMEM((2, tq, 2 * B_V_DIM), F32)],
        compiler_params=_params("parallel", "parallel", "parallel"),
        name="mixer_b_latent" if has_lat else "mixer_b_ctx",
    )(*args)


def _log_sigmoid(x):
    return jnp.minimum(x, 0.0) - jnp.log(1.0 + jnp.exp(-jnp.abs(x)))


def _per_head(ref, row, shape, axis, width):
    idx = lax.broadcasted_iota(jnp.int32, shape, axis) // width
    out = jnp.zeros(shape, F32)
    for h in range(C_HEADS):
        out = jnp.where(idx == h, ref[row, h], out)
    return out


def _mixer_c_kernel(*refs, n, ctx_len):
    if ctx_len:
        dec_ref, q_ref, k_ref, v_ref, g_ref, kc_ref, vc_ref, o_ref, sf_ref = refs
    else:
        dec_ref, q_ref, k_ref, v_ref, g_ref, o_ref, sf_ref = refs
    ch = C_CHUNK
    nc = n // ch
    qk_w, v_w = C_QK, C_V

    lgf_qk = _log_sigmoid(_per_head(dec_ref, 0, (1, qk_w), 1, C_QK_DIM))
    lgb_qk = _log_sigmoid(_per_head(dec_ref, 1, (1, qk_w), 1, C_QK_DIM))
    lgf_blk = _log_sigmoid(_per_head(dec_ref, 0, (1, C_HEADS * ch), 1, ch))
    lgb_blk = _log_sigmoid(_per_head(dec_ref, 1, (1, C_HEADS * ch), 1, ch))
    lgf_row = _log_sigmoid(_per_head(dec_ref, 0, (qk_w, 1), 0, C_QK_DIM))
    lgb_row = _log_sigmoid(_per_head(dec_ref, 1, (qk_w, 1), 0, C_QK_DIM))

    pos = lax.broadcasted_iota(jnp.int32, (ch, 1), 0).astype(F32)
    q_dec_f = jnp.exp(lgf_qk * (pos + 1.0))
    q_dec_b = jnp.exp(lgb_qk * (ch - pos))
    k_dec_f = jnp.exp(lgf_qk * (ch - 1.0 - pos))
    k_dec_b = jnp.exp(lgb_qk * pos)
    chunk_f = jnp.exp(lgf_row * float(ch))
    chunk_b = jnp.exp(lgb_row * float(ch))
    col = lax.broadcasted_iota(jnp.int32, (ch, C_HEADS * ch), 1) & (ch - 1)
    diff = (lax.broadcasted_iota(jnp.int32, (ch, C_HEADS * ch), 0) - col).astype(F32)
    decay = jnp.where(diff >= 0.0, jnp.exp(lgf_blk * jnp.maximum(diff, 0.0)),
                      jnp.exp(lgb_blk * jnp.maximum(-diff, 0.0)))
    state_mask = (lax.broadcasted_iota(jnp.int32, (qk_w, v_w), 0) // C_QK_DIM
                  == lax.broadcasted_iota(jnp.int32, (qk_w, v_w), 1) // C_V_DIM)
    head_qk = lax.broadcasted_iota(jnp.int32, (ch, qk_w), 1) // C_QK_DIM
    head_v = lax.broadcasted_iota(jnp.int32, (ch, v_w), 1) // C_V_DIM

    def kv_update(kf32, vb16, weights):
        kk = jnp.concatenate([kf32 * w for w in weights], axis=1) if len(weights) > 1 \
            else kf32 * weights[0]
        u = _dot(kk.T.astype(BF16), vb16)
        return [jnp.where(state_mask, u[j * qk_w:(j + 1) * qk_w], 0.0) for j in range(len(weights))]

    if ctx_len:
        cpos = lax.broadcasted_iota(jnp.int32, (ctx_len, 1), 0).astype(F32)
        s_f, s_b = kv_update(kc_ref[0].astype(F32), vc_ref[0],
                             [jnp.exp(lgf_qk * (ctx_len - 1.0 - cpos)), jnp.exp(lgb_qk * cpos)])
    else:
        s_f = jnp.zeros((qk_w, v_w), F32)
        s_b = jnp.zeros((qk_w, v_w), F32)

    def fwd(c, state):
        off = pl.multiple_of(c * ch, ch)
        sf_ref[c] = state.astype(BF16)
        (u_f,) = kv_update(k_ref[0, pl.ds(off, ch), :].astype(F32), v_ref[0, pl.ds(off, ch), :],
                           [k_dec_f])
        return chunk_f * state + u_f

    lax.fori_loop(0, nc, fwd, s_f)

    def bwd(t, state_b):
        c = nc - 1 - t
        off = pl.multiple_of(c * ch, ch)
        q = q_ref[0, pl.ds(off, ch), :]
        k = k_ref[0, pl.ds(off, ch), :]
        v = v_ref[0, pl.ds(off, ch), :]
        zq = jnp.zeros_like(k)
        zv = jnp.zeros_like(v)
        kz = jnp.concatenate([jnp.where(head_qk == h, k, zq) for h in range(C_HEADS)], axis=0)
        vz = jnp.concatenate([jnp.where(head_v == h, v, zv) for h in range(C_HEADS)], axis=0)
        att = (_nt_dot(q, kz) * decay).astype(BF16)
        y = _dot(att, vz)
        qf = q.astype(F32)
        qq = jnp.concatenate([qf * q_dec_f, qf * q_dec_b], axis=1).astype(BF16)
        states = jnp.concatenate([sf_ref[c], state_b.astype(BF16)], axis=0)
        y = y + _dot(qq, states)
        outs = []
        for j in range(v_w // LANES):
            yb = y[:, j * LANES:(j + 1) * LANES]
            lo = lax.broadcasted_iota(jnp.int32, yb.shape, 1) < C_V_DIM
            s_all = jnp.sum(yb, axis=-1, keepdims=True)
            s_lo = jnp.sum(jnp.where(lo, yb, 0.0), axis=-1, keepdims=True)
            yc = yb - jnp.where(lo, s_lo, s_all - s_lo) * (1.0 / C_V_DIM)
            sq = yc * yc
            q_all = jnp.sum(sq, axis=-1, keepdims=True)
            q_lo = jnp.sum(jnp.where(lo, sq, 0.0), axis=-1, keepdims=True)
            var = jnp.where(lo, q_lo, q_all - q_lo) * (1.0 / C_V_DIM)
            outs.append(yc * lax.rsqrt(var + NORM_EPS))
        yn = jnp.concatenate(outs, axis=1)
        gate = g_ref[0, pl.ds(off, ch), :]
        o_ref[0, pl.ds(off, ch), :] = (gate * _sigmoid(gate) * yn).astype(BF16)
        (u_b,) = kv_update(k.astype(F32), v, [k_dec_b])
        return chunk_b * state_b + u_b

    lax.fori_loop(0, nc, bwd, s_b)


def _mixer_c(dec, q, k, v, g, kc, vc):
    bsz, n, _ = q.shape
    ctx_len = kc.shape[1] if kc is not None else 0
    seq = lambda wd: pl.BlockSpec((1, n, wd), lambda b: (b, 0, 0))
    once = lambda wd: pl.BlockSpec((1, n, wd), lambda b: (b, 0, 0), pipeline_mode=pl.Buffered(1))
    in_specs = [pl.BlockSpec(memory_space=pltpu.SMEM), once(C_QK), once(C_QK), once(C_V), once(C_V)]
    args = [dec, q, k, v, g]
    if ctx_len:
        in_specs += [pl.BlockSpec((1, ctx_len, C_QK), lambda b: (b, 0, 0)),
                     pl.BlockSpec((1, ctx_len, C_V), lambda b: (b, 0, 0))]
        args += [kc, vc]
    return pl.pallas_call(
        functools.partial(_mixer_c_kernel, n=n, ctx_len=ctx_len),
        grid=(bsz,),
        in_specs=in_specs,
        out_specs=seq(C_V),
        out_shape=jax.ShapeDtypeStruct((bsz, n, C_V), BF16),
        scratch_shapes=[pltpu.VMEM((n // C_CHUNK, C_QK, C_V), BF16)],
        compiler_params=_params("parallel"),
        name="mixer_c_latent" if ctx_len else "mixer_c_ctx",
    )(*args)


def _outproj_kernel(a_ref, b_ref, c_ref, x_ref, w_ref, g1_ref, n2_ref, sh_ref, sc_ref,
                    xo_ref, h_ref):
    y = (_dot(a_ref[0], w_ref[0:A_Q, :])
         + _dot(b_ref[0], w_ref[A_Q:A_Q + B_V, :])
         + _dot(c_ref[0], w_ref[A_Q + B_V:, :]))
    xn = x_ref[0] + g1_ref[0] * y
    xo_ref[0] = xn
    h = (_rms(xn) * n2_ref[...]) * (1.0 + sc_ref[0]) + sh_ref[0]
    h_ref[0] = h.astype(BF16)


def _out_proj(a, b, c, x, w, g1, n2, sh, sc, tm):
    bsz, n, d = x.shape
    per_b = (lambda bb, i: (bb, 0, 0)) if g1.shape[0] == bsz else (lambda bb, i: (0, 0, 0))
    tile = lambda wd: pl.BlockSpec((1, tm, wd), lambda bb, i: (bb, i, 0))
    vec = pl.BlockSpec((1, 1, d), per_b)
    return pl.pallas_call(
        _outproj_kernel,
        grid=(bsz, n // tm),
        in_specs=[tile(A_Q), tile(B_V), tile(C_V), tile(d),
                  pl.BlockSpec((d, d), lambda bb, i: (0, 0)),
                  vec, pl.BlockSpec((1, d), lambda bb, i: (0, 0)), vec, vec],
        out_specs=[tile(d), tile(d)],
        out_shape=[jax.ShapeDtypeStruct((bsz, n, d), F32), jax.ShapeDtypeStruct((bsz, n, d), BF16)],
        compiler_params=_params("parallel", "parallel"),
        name="out_proj",
    )(a, b, c, x, w, g1, n2.reshape(1, d), sh, sc)


FF_CHUNK = 256


def _ffn_kernel(x_ref, hm_ref, hp_ref, hn_ref, wup_ref, cw_ref, cb_ref, wdn_ref, g2_ref, fg_ref,
                o_ref, hs_ref, act_ref, *, tm, d_ff, final):
    i = pl.program_id(1)
    last = pl.num_programs(1) - 1
    halo = SUBLANES * 2
    prev = hp_ref[0]
    nxt = hn_ref[0]
    hs_ref[0:halo, :] = jnp.where(i > 0, prev, jnp.zeros_like(prev))
    hs_ref[halo:halo + tm, :] = hm_ref[0]
    hs_ref[halo + tm:2 * halo + tm, :] = jnp.where(i < last, nxt, jnp.zeros_like(nxt))
    hs = hs_ref[...]

    def conv(u, off):
        w = cw_ref[:, off:off + FF_CHUNK]
        return (u[halo - 1:halo - 1 + tm] * w[0:1] + u[halo:halo + tm] * w[1:2]
                + u[halo + 1:halo + 1 + tm] * w[2:3] + cb_ref[:, off:off + FF_CHUNK])

    for c in range(d_ff // FF_CHUNK):
        lo = c * FF_CHUNK
        val = conv(_dot(hs, wup_ref[:, lo:lo + FF_CHUNK]), lo)
        gate = conv(_dot(hs, wup_ref[:, d_ff + lo:d_ff + lo + FF_CHUNK]), d_ff + lo)
        act_ref[:, lo:lo + FF_CHUNK] = (gate * _sigmoid(gate) * val).astype(BF16)
    y = _dot(act_ref[...], wdn_ref[...])
    out = x_ref[0] + g2_ref[0] * y
    if final:
        out = _rms(out) * fg_ref[...]
    o_ref[0] = out


def _ffn(x, h, w_up, conv_w, conv_b, w_down, g2, final_g, tm, final):
    bsz, n, d = x.shape
    d_ff = w_down.shape[0]
    halo = SUBLANES * 2
    nb = tm // halo
    n_halo_blocks = n // halo
    per_b = (lambda bb, i: (bb, 0, 0)) if g2.shape[0] == bsz else (lambda bb, i: (0, 0, 0))
    const = lambda shape: pl.BlockSpec(shape, lambda bb, i: (0, 0), pipeline_mode=pl.Buffered(1))
    return pl.pallas_call(
        functools.partial(_ffn_kernel, tm=tm, d_ff=d_ff, final=final),
        grid=(bsz, n // tm),
        in_specs=[
            pl.BlockSpec((1, tm, d), lambda bb, i: (bb, i, 0)),
            pl.BlockSpec((1, tm, d), lambda bb, i: (bb, i, 0)),
            pl.BlockSpec((1, halo, d), lambda bb, i: (bb, jnp.maximum(i * nb - 1, 0), 0)),
            pl.BlockSpec((1, halo, d),
                         lambda bb, i: (bb, jnp.minimum((i + 1) * nb, n_halo_blocks - 1), 0)),
            const((d, 2 * d_ff)),
            const((CONV_W, 2 * d_ff)),
            const((1, 2 * d_ff)),
            const((d_ff, d)),
            pl.BlockSpec((1, 1, d), per_b),
            const((1, d)),
        ],
        out_specs=pl.BlockSpec((1, tm, d), lambda bb, i: (bb, i, 0)),
        out_shape=jax.ShapeDtypeStruct((bsz, n, d), F32),
        scratch_shapes=[pltpu.VMEM((tm + 2 * halo, d), BF16), pltpu.VMEM((tm, d_ff), BF16)],
        compiler_params=_params("parallel", "parallel"),
        name="ffn_final" if final else "ffn",
    )(x, h, h, h, w_up, conv_w, conv_b.reshape(1, 2 * d_ff), w_down, g2, final_g.reshape(1, d))


def _rope_tables(n_tok, dim):
    n_rows = n_tok // GRID_W
    rows = jnp.repeat(jnp.arange(n_rows, dtype=F32), GRID_W)
    cols = jnp.tile(jnp.arange(GRID_W, dtype=F32), n_rows)
    n_freq = dim // 4
    inv = ROPE_BASE ** (-jnp.arange(n_freq, dtype=F32) / n_freq)
    ang = jnp.concatenate([rows[:, None] * inv, cols[:, None] * inv], axis=-1)
    cos, sin = jnp.cos(ang), jnp.sin(ang)
    reps = LANES // dim
    cos_t = jnp.tile(jnp.concatenate([cos, cos], axis=-1), (1, reps))
    sin_t = jnp.tile(jnp.concatenate([-sin, sin], axis=-1), (1, reps))
    return cos_t, sin_t


def _tile_rows(n, target):
    t = min(n, target)
    while n % t:
        t //= 2
    return t


def kernel(x, c, ctx, c_ctx, w_mod, b_mod, norm1_g, norm2_g, w_in, w_out, attn_sink,
           diff_lambda, diff_subln_g, ret_decay_logit, w_up, conv_w, conv_b, w_down, final_g):
    bsz, n_lat, d = x.shape
    depth = w_mod.shape[0]
    tabs = _rope_tables(n_lat, HEAD_DIM) + _rope_tables(n_lat, C_QK_DIM)

    rows = -(-(bsz + 1) // SUBLANES) * SUBLANES
    cc = jnp.concatenate([c, c_ctx[None, :], jnp.zeros((rows - bsz - 1, d), F32)], axis=0)
    mod = _modulation(cc, w_mod, b_mod)

    w_in_b, w_out_b = w_in.astype(BF16), w_out.astype(BF16)
    w_up_b, w_down_b = w_up.astype(BF16), w_down.astype(BF16)

    tm_lat = _tile_rows(n_lat, 512)
    tm_ctx = _tile_rows(ctx.shape[1], 512)
    xc = ctx
    for l in range(depth):
        is_last = l == depth - 1
        lam_init = 0.8 - 0.6 * math.exp(-0.3 * l)
        m_lat = [mod[l, :bsz, j * d:(j + 1) * d][:, None, :] for j in range(N_MOD)]
        m_ctx = [mod[l, bsz:bsz + 1, j * d:(j + 1) * d][:, None, :] for j in range(N_MOD)]
        sh1, sc1, g1, sh2, sc2, g2 = m_lat
        csh1, csc1, cg1, csh2, csc2, cg2 = m_ctx

        (qa, ka, va, qb, kb, vb, qr, kr, vr, gr) = _in_proj(
            x, norm1_g[l], sh1, sc1, w_in_b[l], tabs, tm_lat)
        (qa_c, ka_c, va_c, qb_c, kb_c, vb_c, qr_c, kr_c, vr_c, gr_c) = _in_proj(
            xc, norm1_g[l], csh1, csc1, w_in_b[l], None, tm_ctx)

        a_out = _mixer_a(attn_sink[l], qa, ka, va, ka_c, va_c, tm_lat)
        b_out = _mixer_b(diff_lambda[l], diff_subln_g[l], qb, kb_c, vb_c, kb, vb,
                         lam_init, _tile_rows(n_lat, 512), _tile_rows(n_lat, 512))
        c_out = _mixer_c(ret_decay_logit[l], qr, kr, vr, gr, kr_c, vr_c)
        x_mid, h2 = _out_proj(a_out, b_out, c_out, x, w_out_b[l], g1, norm2_g[l], sh2, sc2, tm_lat)
        x = _ffn(x_mid, h2, w_up_b[l], conv_w[l], conv_b[l], w_down_b[l], g2, final_g,
                 tm_lat, is_last)

        if not is_last:
            a_c = _mixer_a(attn_sink[l], qa_c, None, None, ka_c, va_c, tm_ctx)
            b_c = _mixer_b(diff_lambda[l], diff_subln_g[l], qb_c, kb_c, vb_c, None, None,
                           lam_init, tm_ctx, tm_ctx)
            c_c = _mixer_c(ret_decay_logit[l], qr_c, kr_c, vr_c, gr_c, None, None)
            xc_mid, hc2 = _out_proj(a_c, b_c, c_c, xc, w_out_b[l], cg1, norm2_g[l], csh2, csc2,
                                    tm_ctx)
            xc = _ffn(xc_mid, hc2, w_up_b[l], conv_w[l], conv_b[l], w_down_b[l], cg2, final_g,
                      tm_ctx, False)
    return x
```

```python
import functools
import math

import jax
import jax.numpy as jnp
from jax import lax
from jax.experimental import pallas as pl
from jax.experimental.pallas import tpu as pltpu

F32 = jnp.float32
BF16 = jnp.bfloat16

GRID_W = 64
HEAD_DIM = 64
ROPE_BASE = 10000.0
NORM_EPS = 1e-6
NEG_INF = -1e30

A_HEADS = 4
A_KV_HEADS = 2
A_WINDOW = 128
A_BLOCK = 128
B_HEADS = 4
B_QK_DIM = 64
B_V_DIM = 128
C_HEADS = 4
C_QK_DIM = 32
C_V_DIM = 64
C_CHUNK = 128

A_Q = A_HEADS * HEAD_DIM
A_KV = A_KV_HEADS * HEAD_DIM
B_QK = B_HEADS * 2 * B_QK_DIM
B_V = B_HEADS * B_V_DIM
C_QK = C_HEADS * C_QK_DIM
C_V = C_HEADS * C_V_DIM
IN_WIDTH = A_Q + 2 * A_KV + 2 * B_QK + B_V + 2 * C_QK + 2 * C_V
N_MOD = 6
CONV_W = 3

LANES = 128
SUBLANES = 8
VMEM_LIMIT = 56 * 1024 * 1024

OFF_QA = 0
OFF_QB = A_Q + 2 * A_KV
OFF_KB = OFF_QB + B_QK
OFF_VB = OFF_KB + B_QK
OFF_QR = OFF_VB + B_V
OFF_VR = OFF_QR + 2 * C_QK
OFF_GR = OFF_VR + C_V


def _params(*sem):
    return pltpu.CompilerParams(dimension_semantics=sem, vmem_limit_bytes=VMEM_LIMIT)


def _nt_dot(a, b):
    return lax.dot_general(a, b, (((1,), (1,)), ((), ())), preferred_element_type=F32)


def _dot(a, b):
    return jnp.dot(a, b, preferred_element_type=F32)


def _rms(x):
    return x * lax.rsqrt(jnp.mean(x * x, axis=-1, keepdims=True) + NORM_EPS)


def _sigmoid(x):
    return 1.0 / (1.0 + jnp.exp(-x))


def _mod_kernel(c_ref, w_ref, b_ref, o_ref):
    c = c_ref[...]
    a = (c * _sigmoid(c)).astype(BF16)
    o_ref[0] = _dot(a, w_ref[0].astype(BF16)) + b_ref[0]


def _modulation(cc, w_mod, b_mod):
    depth, d, width = w_mod.shape
    rows = cc.shape[0]
    blk = d
    return pl.pallas_call(
        _mod_kernel,
        grid=(depth, width // blk),
        in_specs=[
            pl.BlockSpec((rows, d), lambda l, j: (0, 0)),
            pl.BlockSpec((1, d, blk), lambda l, j: (l, 0, j)),
            pl.BlockSpec((1, 1, blk), lambda l, j: (l, 0, j)),
        ],
        out_specs=pl.BlockSpec((1, rows, blk), lambda l, j: (l, 0, j)),
        out_shape=jax.ShapeDtypeStruct((depth, rows, width), F32),
        compiler_params=_params("arbitrary", "arbitrary"),
        name="modulation",
    )(cc, w_mod, b_mod.reshape(depth, 1, width))


def _rope_block(blk, cos, sin_signed, half):
    lane = lax.broadcasted_iota(jnp.int32, blk.shape, 1)
    first = (lane & (2 * half - 1)) < half
    swapped = jnp.where(first, pltpu.roll(blk, LANES - half, 1), pltpu.roll(blk, half, 1))
    return blk * cos + swapped * sin_signed


def _inproj_kernel(*refs, rope):
    if rope:
        (x_ref, g_ref, sh_ref, sc_ref, w_ref, c64_ref, s64_ref, c32_ref, s32_ref,
         qa_ref, ka_ref, va_ref, qb_ref, kb_ref, vb_ref, qr_ref, kr_ref, vr_ref, gr_ref) = refs
    else:
        (x_ref, g_ref, sh_ref, sc_ref, w_ref,
         qa_ref, ka_ref, va_ref, qb_ref, kb_ref, vb_ref, qr_ref, kr_ref, vr_ref, gr_ref) = refs
    x = x_ref[0]
    h = (_rms(x) * g_ref[...]) * (1.0 + sc_ref[0]) + sh_ref[0]
    h = h.astype(BF16)

    def rot(blk, dim):
        if not rope:
            return blk
        if dim == HEAD_DIM:
            return _rope_block(blk, c64_ref[...], s64_ref[...], HEAD_DIM // 2)
        return _rope_block(blk, c32_ref[...], s32_ref[...], C_QK_DIM // 2)

    a_scale = HEAD_DIM ** -0.5 * math.log2(math.e)
    b_scale = B_QK_DIM ** -0.5 * math.log2(math.e)
    c_scale = C_QK_DIM ** -0.5

    p = _dot(h, w_ref[:, OFF_QA:OFF_QB])
    for j in range(A_Q // LANES):
        qa_ref[0, :, j * LANES:(j + 1) * LANES] = rot(
            p[:, j * LANES:(j + 1) * LANES] * a_scale, HEAD_DIM).astype(BF16)
    ka_ref[0] = rot(p[:, A_Q:A_Q + A_KV], HEAD_DIM).astype(BF16)
    va_ref[0, :, :A_KV] = p[:, A_Q + A_KV:A_Q + 2 * A_KV].astype(BF16)
    va_ref[0, :, A_KV:] = jnp.ones((p.shape[0], A_KV), BF16)

    p = _dot(h, w_ref[:, OFF_QB:OFF_KB])
    for j in range(B_QK // LANES):
        qb_ref[0, :, j * LANES:(j + 1) * LANES] = rot(
            p[:, j * LANES:(j + 1) * LANES] * b_scale, HEAD_DIM).astype(BF16)
    p = _dot(h, w_ref[:, OFF_KB:OFF_VB])
    for j in range(B_QK // LANES):
        kb_ref[0, :, j * LANES:(j + 1) * LANES] = rot(
            p[:, j * LANES:(j + 1) * LANES], HEAD_DIM).astype(BF16)
    p = _dot(h, w_ref[:, OFF_VB:OFF_QR]).astype(BF16)
    for j in range(B_HEADS):
        vb_ref[0, :, 2 * j * B_V_DIM:(2 * j + 1) * B_V_DIM] = p[:, j * B_V_DIM:(j + 1) * B_V_DIM]
        vb_ref[0, :, (2 * j + 1) * B_V_DIM:(2 * j + 2) * B_V_DIM] = jnp.ones(
            (p.shape[0], B_V_DIM), BF16)

    p = _dot(h, w_ref[:, OFF_QR:OFF_VR])
    qr_ref[0] = rot(p[:, :C_QK], C_QK_DIM).astype(BF16)
    kr_ref[0] = rot(p[:, C_QK:] * c_scale, C_QK_DIM).astype(BF16)
    vr_ref[0] = _dot(h, w_ref[:, OFF_VR:OFF_GR]).astype(BF16)
    gr_ref[0] = _dot(h, w_ref[:, OFF_GR:IN_WIDTH])


def _in_proj(x, g, sh, sc, w, tabs, tm):
    bsz, n, d = x.shape
    rope = tabs is not None
    per_b = (lambda b, i: (b, 0, 0)) if sh.shape[0] == bsz else (lambda b, i: (0, 0, 0))
    in_specs = [
        pl.BlockSpec((1, tm, d), lambda b, i: (b, i, 0)),
        pl.BlockSpec((1, d), lambda b, i: (0, 0)),
        pl.BlockSpec((1, 1, d), per_b),
        pl.BlockSpec((1, 1, d), per_b),
        pl.BlockSpec((d, IN_WIDTH), lambda b, i: (0, 0)),
    ]
    args = [x, g.reshape(1, d), sh, sc, w]
    if rope:
        in_specs += [pl.BlockSpec((tm, LANES), lambda b, i: (i, 0))] * 4
        args += list(tabs)
    widths = (A_Q, A_KV, 2 * A_KV, B_QK, B_QK, 2 * B_V, C_QK, C_QK, C_V, C_V)
    dtypes = (BF16,) * 9 + (F32,)
    out_specs = [pl.BlockSpec((1, tm, wd), lambda b, i: (b, i, 0)) for wd in widths]
    out_shape = [jax.ShapeDtypeStruct((bsz, n, wd), dt) for wd, dt in zip(widths, dtypes)]
    return pl.pallas_call(
        functools.partial(_inproj_kernel, rope=rope),
        grid=(bsz, n // tm),
        in_specs=in_specs,
        out_specs=out_specs,
        out_shape=out_shape,
        compiler_params=_params("parallel", "parallel"),
        name="in_proj_rope" if rope else "in_proj_ctx",
    )(*args)


def _mixer_a_kernel(*refs, tq, n_lat, local):
    if local:
        sink_ref, q_ref, k_ref, v_ref, kc_ref, vc_ref, o_ref = refs
    else:
        sink_ref, q_ref, kc_ref, vc_ref, o_ref = refs
    blk = A_BLOCK
    half = HEAD_DIM
    lane = lax.broadcasted_iota(jnp.int32, (blk, LANES), 1)
    low = lane < half
    row2 = lax.broadcasted_iota(jnp.int32, (2 * blk, 1), 0)
    kc = kc_ref[0]
    vc = vc_ref[0]
    log2e = math.log2(math.e)
    if local:
        nb = n_lat // blk
        qrow = lax.broadcasted_iota(jnp.int32, (2 * blk, blk), 0) & (blk - 1)
        kcol = lax.broadcasted_iota(jnp.int32, (2 * blk, blk), 1)
        tri_prev = kcol >= qrow
        tri_next = kcol <= qrow
    for sub in range(tq // blk):
        r0 = sub * blk
        gb = pl.program_id(1) * (tq // blk) + sub
        if local:
            prev0 = pl.multiple_of(jnp.maximum(gb - 1, 0) * blk, blk)
            cur0 = pl.multiple_of(gb * blk, blk)
            next0 = pl.multiple_of(jnp.minimum(gb + 1, nb - 1) * blk, blk)
            k_all = jnp.concatenate([k_ref[0, pl.ds(prev0, blk), :], k_ref[0, pl.ds(cur0, blk), :],
                                     k_ref[0, pl.ds(next0, blk), :], kc], axis=0)
            v_all = jnp.concatenate([v_ref[0, pl.ds(prev0, blk), :], v_ref[0, pl.ds(cur0, blk), :],
                                     v_ref[0, pl.ds(next0, blk), :], vc], axis=0)
            ok_prev = jnp.logical_and(tri_prev, gb > 0)
            ok_next = jnp.logical_and(tri_next, gb < nb - 1)
        else:
            k_all, v_all = kc, vc
        for hk in range(A_KV_HEADS):
            qp = q_ref[0, r0:r0 + blk, hk * LANES:(hk + 1) * LANES].astype(F32)
            qsw = pltpu.roll(qp, half, 1)
            if hk == 0:
                q0, q1 = jnp.where(low, qp, 0.0), jnp.where(low, qsw, 0.0)
            else:
                q0, q1 = jnp.where(low, 0.0, qsw), jnp.where(low, 0.0, qp)
            qz = jnp.concatenate([q0, q1], axis=0).astype(BF16)
            sink = jnp.where(row2 < blk, sink_ref[2 * hk], sink_ref[2 * hk + 1]) * log2e
            s = _nt_dot(qz, k_all)
            if local:
                s = jnp.concatenate([jnp.where(ok_prev, s[:, :blk], NEG_INF), s[:, blk:2 * blk],
                                     jnp.where(ok_next, s[:, 2 * blk:3 * blk], NEG_INF),
                                     s[:, 3 * blk:]], axis=1)
            m = jnp.maximum(jnp.max(s, axis=-1, keepdims=True), sink)
            r = _dot(jnp.exp2(s - m).astype(BF16), v_all)
            r = r[:, :LANES] / (r[:, LANES:] + jnp.exp2(sink - m))
            ra, rb = r[:blk], r[blk:]
            if hk == 0:
                pair = jnp.where(low, ra, pltpu.roll(rb, half, 1))
            else:
                pair = jnp.where(low, pltpu.roll(ra, half, 1), rb)
            o_ref[0, r0:r0 + blk, hk * LANES:(hk + 1) * LANES] = pair.astype(BF16)


def _mixer_a(sink, q, k, v, kc, vc, tq):
    bsz, n, _ = q.shape
    ctx_len = kc.shape[1]
    local = k is not None
    in_specs = [pl.BlockSpec(memory_space=pltpu.SMEM),
                pl.BlockSpec((1, tq, A_Q), lambda b, i: (b, i, 0))]
    args = [sink, q]
    if local:
        in_specs += [pl.BlockSpec((1, n, A_KV), lambda b, i: (b, 0, 0)),
                     pl.BlockSpec((1, n, 2 * A_KV), lambda b, i: (b, 0, 0))]
        args += [k, v]
    in_specs += [pl.BlockSpec((1, ctx_len, A_KV), lambda b, i: (b, 0, 0)),
                 pl.BlockSpec((1, ctx_len, 2 * A_KV), lambda b, i: (b, 0, 0))]
    args += [kc, vc]
    return pl.pallas_call(
        functools.partial(_mixer_a_kernel, tq=tq, n_lat=n, local=local),
        grid=(bsz, n // tq),
        in_specs=in_specs,
        out_specs=pl.BlockSpec((1, tq, A_Q), lambda b, i: (b, i, 0)),
        out_shape=jax.ShapeDtypeStruct((bsz, n, A_Q), BF16),
        compiler_params=_params("parallel", "parallel"),
        name="mixer_a_window" if local else "mixer_a_ctx",
    )(*args)


def _mixer_b_kernel(*refs, tq, tk, n_lat, lam_init):
    if n_lat:
        lam_ref, g_ref, q_ref, kc_ref, vc_ref, k_ref, v_ref, o_ref, acc_ref = refs
    else:
        lam_ref, g_ref, q_ref, kc_ref, vc_ref, o_ref, acc_ref = refs
    q = q_ref[0]
    lane = lax.broadcasted_iota(jnp.int32, q.shape, 1)
    zero = jnp.zeros_like(q)
    qz = (jnp.where(lane < B_QK_DIM, q, zero), jnp.where(lane < B_QK_DIM, zero, q))

    def step(kblk, vblk, ms, first):
        out = []
        for c in range(2):
            s = _nt_dot(qz[c], kblk)
            m_new = jnp.maximum(ms[c], jnp.max(s, axis=-1, keepdims=True))
            p = jnp.exp2(s - m_new)
            pv = _dot(p.astype(BF16), vblk)
            if first:
                acc_ref[c] = pv
            else:
                acc_ref[c] = jnp.exp2(ms[c] - m_new) * acc_ref[c] + pv
            out.append(m_new)
        return tuple(out)

    m0 = jnp.full((tq, 1), NEG_INF, F32)
    ms = step(kc_ref[0], vc_ref[0], (m0, m0), True)
    for j in range(n_lat // tk):
        ms = step(k_ref[0, j * tk:(j + 1) * tk, :], v_ref[0, j * tk:(j + 1) * tk, :], ms, False)

    d = lam_ref[...]
    lam = (jnp.exp(jnp.sum(d[0:1] * d[1:2], axis=-1, keepdims=True))
           - jnp.exp(jnp.sum(d[2:3] * d[3:4], axis=-1, keepdims=True)) + lam_init)
    o = (acc_ref[0, :, :B_V_DIM] / acc_ref[0, :, B_V_DIM:]
         - lam * (acc_ref[1, :, :B_V_DIM] / acc_ref[1, :, B_V_DIM:]))
    y = _rms(o) * g_ref[...] * (1.0 - lam_init)
    o_ref[0] = y.astype(BF16)


def _mixer_b(lam_p, g, q, kc, vc, k, v, lam_init, tq, tk):
    bsz, n, _ = q.shape
    ctx_len = kc.shape[1]
    has_lat = k is not None
    n_lat = k.shape[1] if has_lat else 0
    in_specs = [
        pl.BlockSpec((4, B_QK_DIM), lambda b, h, i: (0, 0)),
        pl.BlockSpec((1, B_V_DIM), lambda b, h, i: (0, 0)),
        pl.BlockSpec((1, tq, LANES), lambda b, h, i: (b, i, h)),
        pl.BlockSpec((1, ctx_len, LANES), lambda b, h, i: (b, 0, h)),
        pl.BlockSpec((1, ctx_len, 2 * B_V_DIM), lambda b, h, i: (b, 0, h)),
    ]
    args = [lam_p, g.reshape(1, B_V_DIM), q, kc, vc]
    if has_lat:
        in_specs += [pl.BlockSpec((1, n_lat, LANES), lambda b, h, i: (b, 0, h)),
                     pl.BlockSpec((1, n_lat, 2 * B_V_DIM), lambda b, h, i: (b, 0, h))]
        args += [k, v]
    return pl.pallas_call(
        functools.partial(_mixer_b_kernel, tq=tq, tk=tk, n_lat=n_lat, lam_init=lam_init),
        grid=(bsz, B_HEADS, n // tq),
        in_specs=in_specs,
        out_specs=pl.BlockSpec((1, tq, LANES), lambda b, h, i: (b, i, h)),
        out_shape=jax.ShapeDtypeStruct((bsz, n, B_V), BF16),
        scratch_shapes=[pltpu.VMEM((2, tq, 2 * B_V_DIM), F32)],
        compiler_params=_params("parallel", "parallel", "parallel"),
        name="mixer_b_latent" if has_lat else "mixer_b_ctx",
    )(*args)


def _log_sigmoid(x):
    return jnp.minimum(x, 0.0) - jnp.log(1.0 + jnp.exp(-jnp.abs(x)))


def _per_head(ref, row, shape, axis, width):
    idx = lax.broadcasted_iota(jnp.int32, shape, axis) // width
    out = jnp.zeros(shape, F32)
    for h in range(C_HEADS):
        out = jnp.where(idx == h, ref[row, h], out)
    return out


def _mixer_c_kernel(*refs, n, ctx_len):
    if ctx_len:
        dec_ref, q_ref, k_ref, v_ref, g_ref, kc_ref, vc_ref, o_ref, sf_ref = refs
    else:
        dec_ref, q_ref, k_ref, v_ref, g_ref, o_ref, sf_ref = refs
    ch = C_CHUNK
    nc = n // ch
    qk_w, v_w = C_QK, C_V

    lgf_qk = _log_sigmoid(_per_head(dec_ref, 0, (1, qk_w), 1, C_QK_DIM))
    lgb_qk = _log_sigmoid(_per_head(dec_ref, 1, (1, qk_w), 1, C_QK_DIM))
    lgf_blk = _log_sigmoid(_per_head(dec_ref, 0, (1, C_HEADS * ch), 1, ch))
    lgb_blk = _log_sigmoid(_per_head(dec_ref, 1, (1, C_HEADS * ch), 1, ch))
    lgf_row = _log_sigmoid(_per_head(dec_ref, 0, (qk_w, 1), 0, C_QK_DIM))
    lgb_row = _log_sigmoid(_per_head(dec_ref, 1, (qk_w, 1), 0, C_QK_DIM))

    pos = lax.broadcasted_iota(jnp.int32, (ch, 1), 0).astype(F32)
    q_dec_f = jnp.exp(lgf_qk * (pos + 1.0))
    q_dec_b = jnp.exp(lgb_qk * (ch - pos))
    k_dec_f = jnp.exp(lgf_qk * (ch - 1.0 - pos))
    k_dec_b = jnp.exp(lgb_qk * pos)
    chunk_f = jnp.exp(lgf_row * float(ch))
    chunk_b = jnp.exp(lgb_row * float(ch))
    col = lax.broadcasted_iota(jnp.int32, (ch, C_HEADS * ch), 1) & (ch - 1)
    diff = (lax.broadcasted_iota(jnp.int32, (ch, C_HEADS * ch), 0) - col).astype(F32)
    decay = jnp.where(diff >= 0.0, jnp.exp(lgf_blk * jnp.maximum(diff, 0.0)),
                      jnp.exp(lgb_blk * jnp.maximum(-diff, 0.0)))
    state_mask = (lax.broadcasted_iota(jnp.int32, (qk_w, v_w), 0) // C_QK_DIM
                  == lax.broadcasted_iota(jnp.int32, (qk_w, v_w), 1) // C_V_DIM)
    head_qk = lax.broadcasted_iota(jnp.int32, (ch, qk_w), 1) // C_QK_DIM
    head_v = lax.broadcasted_iota(jnp.int32, (ch, v_w), 1) // C_V_DIM

    def kv_update(kf32, vb16, weights):
        kk = jnp.concatenate([kf32 * w for w in weights], axis=1) if len(weights) > 1 \
            else kf32 * weights[0]
        u = _dot(kk.T.astype(BF16), vb16)
        return [jnp.where(state_mask, u[j * qk_w:(j + 1) * qk_w], 0.0) for j in range(len(weights))]

    if ctx_len:
        cpos = lax.broadcasted_iota(jnp.int32, (ctx_len, 1), 0).astype(F32)
        s_f, s_b = kv_update(kc_ref[0].astype(F32), vc_ref[0],
                             [jnp.exp(lgf_qk * (ctx_len - 1.0 - cpos)), jnp.exp(lgb_qk * cpos)])
    else:
        s_f = jnp.zeros((qk_w, v_w), F32)
        s_b = jnp.zeros((qk_w, v_w), F32)

    def fwd(c, state):
        off = pl.multiple_of(c * ch, ch)
        sf_ref[c] = state.astype(BF16)
        (u_f,) = kv_update(k_ref[0, pl.ds(off, ch), :].astype(F32), v_ref[0, pl.ds(off, ch), :],
                           [k_dec_f])
        return chunk_f * state + u_f

    lax.fori_loop(0, nc, fwd, s_f, unroll=min(nc, 4))

    def bwd(t, state_b):
        c = nc - 1 - t
        off = pl.multiple_of(c * ch, ch)
        q = q_ref[0, pl.ds(off, ch), :]
        k = k_ref[0, pl.ds(off, ch), :]
        v = v_ref[0, pl.ds(off, ch), :]
        zq = jnp.zeros_like(k)
        zv = jnp.zeros_like(v)
        kz = jnp.concatenate([jnp.where(head_qk == h, k, zq) for h in range(C_HEADS)], axis=0)
        vz = jnp.concatenate([jnp.where(head_v == h, v, zv) for h in range(C_HEADS)], axis=0)
        att = (_nt_dot(q, kz) * decay).astype(BF16)
        y = _dot(att, vz)
        qf = q.astype(F32)
        qq = jnp.concatenate([qf * q_dec_f, qf * q_dec_b], axis=1).astype(BF16)
        states = jnp.concatenate([sf_ref[c], state_b.astype(BF16)], axis=0)
        y = y + _dot(qq, states)
        outs = []
        for j in range(v_w // LANES):
            yb = y[:, j * LANES:(j + 1) * LANES]
            lo = lax.broadcasted_iota(jnp.int32, yb.shape, 1) < C_V_DIM
            s_all = jnp.sum(yb, axis=-1, keepdims=True)
            s_lo = jnp.sum(jnp.where(lo, yb, 0.0), axis=-1, keepdims=True)
            yc = yb - jnp.where(lo, s_lo, s_all - s_lo) * (1.0 / C_V_DIM)
            sq = yc * yc
            q_all = jnp.sum(sq, axis=-1, keepdims=True)
            q_lo = jnp.sum(jnp.where(lo, sq, 0.0), axis=-1, keepdims=True)
            var = jnp.where(lo, q_lo, q_all - q_lo) * (1.0 / C_V_DIM)
            outs.append(yc * lax.rsqrt(var + NORM_EPS))
        yn = jnp.concatenate(outs, axis=1)
        gate = g_ref[0, pl.ds(off, ch), :]
        o_ref[0, pl.ds(off, ch), :] = (gate * _sigmoid(gate) * yn).astype(BF16)
        (u_b,) = kv_update(k.astype(F32), v, [k_dec_b])
        return chunk_b * state_b + u_b

    lax.fori_loop(0, nc, bwd, s_b, unroll=min(nc, 2))


def _mixer_c(dec, q, k, v, g, kc, vc):
    bsz, n, _ = q.shape
    ctx_len = kc.shape[1] if kc is not None else 0
    seq = lambda wd: pl.BlockSpec((1, n, wd), lambda b: (b, 0, 0))
    once = lambda wd: pl.BlockSpec((1, n, wd), lambda b: (b, 0, 0), pipeline_mode=pl.Buffered(1))
    in_specs = [pl.BlockSpec(memory_space=pltpu.SMEM), once(C_QK), once(C_QK), once(C_V), once(C_V)]
    args = [dec, q, k, v, g]
    if ctx_len:
        in_specs += [pl.BlockSpec((1, ctx_len, C_QK), lambda b: (b, 0, 0)),
                     pl.BlockSpec((1, ctx_len, C_V), lambda b: (b, 0, 0))]
        args += [kc, vc]
    return pl.pallas_call(
        functools.partial(_mixer_c_kernel, n=n, ctx_len=ctx_len),
        grid=(bsz,),
        in_specs=in_specs,
        out_specs=seq(C_V),
        out_shape=jax.ShapeDtypeStruct((bsz, n, C_V), BF16),
        scratch_shapes=[pltpu.VMEM((n // C_CHUNK, C_QK, C_V), BF16)],
        compiler_params=_params("parallel"),
        name="mixer_c_latent" if ctx_len else "mixer_c_ctx",
    )(*args)


def _outproj_kernel(a_ref, b_ref, c_ref, x_ref, w_ref, g1_ref, n2_ref, sh_ref, sc_ref,
                    xo_ref, h_ref):
    y = (_dot(a_ref[0], w_ref[0:A_Q, :])
         + _dot(b_ref[0], w_ref[A_Q:A_Q + B_V, :])
         + _dot(c_ref[0], w_ref[A_Q + B_V:, :]))
    xn = x_ref[0] + g1_ref[0] * y
    xo_ref[0] = xn
    h = (_rms(xn) * n2_ref[...]) * (1.0 + sc_ref[0]) + sh_ref[0]
    h_ref[0] = h.astype(BF16)


def _out_proj(a, b, c, x, w, g1, n2, sh, sc, tm):
    bsz, n, d = x.shape
    per_b = (lambda bb, i: (bb, 0, 0)) if g1.shape[0] == bsz else (lambda bb, i: (0, 0, 0))
    tile = lambda wd: pl.BlockSpec((1, tm, wd), lambda bb, i: (bb, i, 0))
    vec = pl.BlockSpec((1, 1, d), per_b)
    return pl.pallas_call(
        _outproj_kernel,
        grid=(bsz, n // tm),
        in_specs=[tile(A_Q), tile(B_V), tile(C_V), tile(d),
                  pl.BlockSpec((d, d), lambda bb, i: (0, 0)),
                  vec, pl.BlockSpec((1, d), lambda bb, i: (0, 0)), vec, vec],
        out_specs=[tile(d), tile(d)],
        out_shape=[jax.ShapeDtypeStruct((bsz, n, d), F32), jax.ShapeDtypeStruct((bsz, n, d), BF16)],
        compiler_params=_params("parallel", "parallel"),
        name="out_proj",
    )(a, b, c, x, w, g1, n2.reshape(1, d), sh, sc)


FF_CHUNK = 256


def _ffn_kernel(x_ref, hm_ref, hp_ref, hn_ref, wup_ref, cw_ref, cb_ref, wdn_ref, g2_ref, fg_ref,
                o_ref, hs_ref, act_ref, *, tm, d_ff, final):
    i = pl.program_id(1)
    last = pl.num_programs(1) - 1
    halo = SUBLANES * 2
    prev = hp_ref[0]
    nxt = hn_ref[0]
    hs_ref[0:halo, :] = jnp.where(i > 0, prev, jnp.zeros_like(prev))
    hs_ref[halo:halo + tm, :] = hm_ref[0]
    hs_ref[halo + tm:2 * halo + tm, :] = jnp.where(i < last, nxt, jnp.zeros_like(nxt))
    hs = hs_ref[...]

    def conv(u, off):
        w = cw_ref[:, off:off + FF_CHUNK]
        return (u[halo - 1:halo - 1 + tm] * w[0:1] + u[halo:halo + tm] * w[1:2]
                + u[halo + 1:halo + 1 + tm] * w[2:3] + cb_ref[:, off:off + FF_CHUNK])

    for c in range(d_ff // FF_CHUNK):
        lo = c * FF_CHUNK
        val = conv(_dot(hs, wup_ref[:, lo:lo + FF_CHUNK]), lo)
        gate = conv(_dot(hs, wup_ref[:, d_ff + lo:d_ff + lo + FF_CHUNK]), d_ff + lo)
        act_ref[:, lo:lo + FF_CHUNK] = (gate * _sigmoid(gate) * val).astype(BF16)
    y = _dot(act_ref[...], wdn_ref[...])
    out = x_ref[0] + g2_ref[0] * y
    if final:
        out = _rms(out) * fg_ref[...]
    o_ref[0] = out


def _ffn(x, h, w_up, conv_w, conv_b, w_down, g2, final_g, tm, final):
    bsz, n, d = x.shape
    d_ff = w_down.shape[0]
    halo = SUBLANES * 2
    nb = tm // halo
    n_halo_blocks = n // halo
    per_b = (lambda bb, i: (bb, 0, 0)) if g2.shape[0] == bsz else (lambda bb, i: (0, 0, 0))
    const = lambda shape: pl.BlockSpec(shape, lambda bb, i: (0, 0), pipeline_mode=pl.Buffered(1))
    return pl.pallas_call(
        functools.partial(_ffn_kernel, tm=tm, d_ff=d_ff, final=final),
        grid=(bsz, n // tm),
        in_specs=[
            pl.BlockSpec((1, tm, d), lambda bb, i: (bb, i, 0)),
            pl.BlockSpec((1, tm, d), lambda bb, i: (bb, i, 0)),
            pl.BlockSpec((1, halo, d), lambda bb, i: (bb, jnp.maximum(i * nb - 1, 0), 0)),
            pl.BlockSpec((1, halo, d),
                         lambda bb, i: (bb, jnp.minimum((i + 1) * nb, n_halo_blocks - 1), 0)),
            const((d, 2 * d_ff)),
            const((CONV_W, 2 * d_ff)),
            const((1, 2 * d_ff)),
            const((d_ff, d)),
            pl.BlockSpec((1, 1, d), per_b),
            const((1, d)),
        ],
        out_specs=pl.BlockSpec((1, tm, d), lambda bb, i: (bb, i, 0)),
        out_shape=jax.ShapeDtypeStruct((bsz, n, d), F32),
        scratch_shapes=[pltpu.VMEM((tm + 2 * halo, d), BF16), pltpu.VMEM((tm, d_ff), BF16)],
        compiler_params=_params("parallel", "parallel"),
        name="ffn_final" if final else "ffn",
    )(x, h, h, h, w_up, conv_w, conv_b.reshape(1, 2 * d_ff), w_down, g2, final_g.reshape(1, d))


def _rope_tables(n_tok, dim):
    n_rows = n_tok // GRID_W
    rows = jnp.repeat(jnp.arange(n_rows, dtype=F32), GRID_W)
    cols = jnp.tile(jnp.arange(GRID_W, dtype=F32), n_rows)
    n_freq = dim // 4
    inv = ROPE_BASE ** (-jnp.arange(n_freq, dtype=F32) / n_freq)
    ang = jnp.concatenate([rows[:, None] * inv, cols[:, None] * inv], axis=-1)
    cos, sin = jnp.cos(ang), jnp.sin(ang)
    reps = LANES // dim
    cos_t = jnp.tile(jnp.concatenate([cos, cos], axis=-1), (1, reps))
    sin_t = jnp.tile(jnp.concatenate([-sin, sin], axis=-1), (1, reps))
    return cos_t, sin_t


def _tile_rows(n, target):
    t = min(n, target)
    while n % t:
        t //= 2
    return t


def kernel(x, c, ctx, c_ctx, w_mod, b_mod, norm1_g, norm2_g, w_in, w_out, attn_sink,
           diff_lambda, diff_subln_g, ret_decay_logit, w_up, conv_w, conv_b, w_down, final_g):
    bsz, n_lat, d = x.shape
    depth = w_mod.shape[0]
    tabs = _rope_tables(n_lat, HEAD_DIM) + _rope_tables(n_lat, C_QK_DIM)

    rows = -(-(bsz + 1) // SUBLANES) * SUBLANES
    cc = jnp.concatenate([c, c_ctx[None, :], jnp.zeros((rows - bsz - 1, d), F32)], axis=0)
    mod = _modulation(cc, w_mod, b_mod)

    w_in_b = [w_in[l].astype(BF16) for l in range(depth)]
    w_out_b = [w_out[l].astype(BF16) for l in range(depth)]
    w_up_b = [w_up[l].astype(BF16) for l in range(depth)]
    w_down_b = [w_down[l].astype(BF16) for l in range(depth)]

    tm_lat = _tile_rows(n_lat, 512)
    tm_ctx = _tile_rows(ctx.shape[1], 512)
    xc = ctx
    for l in range(depth):
        is_last = l == depth - 1
        lam_init = 0.8 - 0.6 * math.exp(-0.3 * l)
        m_lat = [mod[l, :bsz, j * d:(j + 1) * d][:, None, :] for j in range(N_MOD)]
        m_ctx = [mod[l, bsz:bsz + 1, j * d:(j + 1) * d][:, None, :] for j in range(N_MOD)]
        sh1, sc1, g1, sh2, sc2, g2 = m_lat
        csh1, csc1, cg1, csh2, csc2, cg2 = m_ctx

        (qa, ka, va, qb, kb, vb, qr, kr, vr, gr) = _in_proj(
            x, norm1_g[l], sh1, sc1, w_in_b[l], tabs, tm_lat)
        (qa_c, ka_c, va_c, qb_c, kb_c, vb_c, qr_c, kr_c, vr_c, gr_c) = _in_proj(
            xc, norm1_g[l], csh1, csc1, w_in_b[l], None, tm_ctx)

        a_out = _mixer_a(attn_sink[l], qa, ka, va, ka_c, va_c, tm_lat)
        b_out = _mixer_b(diff_lambda[l], diff_subln_g[l], qb, kb_c, vb_c, kb, vb,
                         lam_init, _tile_rows(n_lat, 512), _tile_rows(n_lat, 512))
        c_out = _mixer_c(ret_decay_logit[l], qr, kr, vr, gr, kr_c, vr_c)
        x_mid, h2 = _out_proj(a_out, b_out, c_out, x, w_out_b[l], g1, norm2_g[l], sh2, sc2, tm_lat)
        x = _ffn(x_mid, h2, w_up_b[l], conv_w[l], conv_b[l], w_down_b[l], g2, final_g,
                 tm_lat, is_last)

        if not is_last:
            a_c = _mixer_a(attn_sink[l], qa_c, None, None, ka_c, va_c, tm_ctx)
            b_c = _mixer_b(diff_lambda[l], diff_subln_g[l], qb_c, kb_c, vb_c, None, None,
                           lam_init, tm_ctx, tm_ctx)
            c_c = _mixer_c(ret_decay_logit[l], qr_c, kr_c, vr_c, gr_c, None, None)
            xc_mid, hc2 = _out_proj(a_c, b_c, c_c, xc, w_out_b[l], cg1, norm2_g[l], csh2, csc2,
                                    tm_ctx)
            xc = _ffn(xc_mid, hc2, w_up_b[l], conv_w[l], conv_b[l], w_down_b[l], cg2, final_g,
                      tm_ctx, False)
    return x
```

```python
import functools
import math

import jax
import jax.numpy as jnp
from jax import lax
from jax.experimental import pallas as pl
from jax.experimental.pallas import tpu as pltpu

F32 = jnp.float32
BF16 = jnp.bfloat16

GRID_W = 64
HEAD_DIM = 64
ROPE_BASE = 10000.0
NORM_EPS = 1e-6
NEG_INF = -1e30

A_HEADS = 4
A_KV_HEADS = 2
A_WINDOW = 128
A_BLOCK = 128
B_HEADS = 4
B_QK_DIM = 64
B_V_DIM = 128
C_HEADS = 4
C_QK_DIM = 32
C_V_DIM = 64
C_CHUNK = 128

A_Q = A_HEADS * HEAD_DIM
A_KV = A_KV_HEADS * HEAD_DIM
B_QK = B_HEADS * 2 * B_QK_DIM
B_V = B_HEADS * B_V_DIM
C_QK = C_HEADS * C_QK_DIM
C_V = C_HEADS * C_V_DIM
IN_WIDTH = A_Q + 2 * A_KV + 2 * B_QK + B_V + 2 * C_QK + 2 * C_V
N_MOD = 6
CONV_W = 3

LANES = 128
SUBLANES = 8
VMEM_LIMIT = 56 * 1024 * 1024

OFF_QA = 0
OFF_QB = A_Q + 2 * A_KV
OFF_KB = OFF_QB + B_QK
OFF_VB = OFF_KB + B_QK
OFF_QR = OFF_VB + B_V
OFF_VR = OFF_QR + 2 * C_QK
OFF_GR = OFF_VR + C_V


def _params(*sem):
    return pltpu.CompilerParams(dimension_semantics=sem, vmem_limit_bytes=VMEM_LIMIT)


def _nt_dot(a, b):
    return lax.dot_general(a, b, (((1,), (1,)), ((), ())), preferred_element_type=F32)


def _dot(a, b):
    return jnp.dot(a, b, preferred_element_type=F32)


def _rms(x):
    return x * lax.rsqrt(jnp.mean(x * x, axis=-1, keepdims=True) + NORM_EPS)


def _sigmoid(x):
    return 1.0 / (1.0 + jnp.exp(-x))


def _mod_kernel(c_ref, w_ref, b_ref, o_ref):
    c = c_ref[...]
    a = (c * _sigmoid(c)).astype(BF16)
    o_ref[0] = _dot(a, w_ref[0].astype(BF16)) + b_ref[0]


def _modulation(cc, w_mod, b_mod):
    depth, d, width = w_mod.shape
    rows = cc.shape[0]
    blk = d
    return pl.pallas_call(
        _mod_kernel,
        grid=(depth, width // blk),
        in_specs=[
            pl.BlockSpec((rows, d), lambda l, j: (0, 0)),
            pl.BlockSpec((1, d, blk), lambda l, j: (l, 0, j)),
            pl.BlockSpec((1, 1, blk), lambda l, j: (l, 0, j)),
        ],
        out_specs=pl.BlockSpec((1, rows, blk), lambda l, j: (l, 0, j)),
        out_shape=jax.ShapeDtypeStruct((depth, rows, width), F32),
        compiler_params=_params("arbitrary", "arbitrary"),
        name="modulation",
    )(cc, w_mod, b_mod.reshape(depth, 1, width))


def _rope_block(blk, cos, sin_signed, half):
    lane = lax.broadcasted_iota(jnp.int32, blk.shape, 1)
    first = (lane & (2 * half - 1)) < half
    swapped = jnp.where(first, pltpu.roll(blk, LANES - half, 1), pltpu.roll(blk, half, 1))
    return blk * cos + swapped * sin_signed


def _inproj_kernel(*refs, rope):
    if rope:
        (x_ref, g_ref, sh_ref, sc_ref, w_ref, c64_ref, s64_ref, c32_ref, s32_ref,
         qa_ref, ka_ref, va_ref, qb_ref, kb_ref, vb_ref, qr_ref, kr_ref, vr_ref, gr_ref) = refs
    else:
        (x_ref, g_ref, sh_ref, sc_ref, w_ref,
         qa_ref, ka_ref, va_ref, qb_ref, kb_ref, vb_ref, qr_ref, kr_ref, vr_ref, gr_ref) = refs
    x = x_ref[0]
    h = (_rms(x) * g_ref[...]) * (1.0 + sc_ref[0]) + sh_ref[0]
    h = h.astype(BF16)

    def rot(blk, dim):
        if not rope:
            return blk
        if dim == HEAD_DIM:
            return _rope_block(blk, c64_ref[...], s64_ref[...], HEAD_DIM // 2)
        return _rope_block(blk, c32_ref[...], s32_ref[...], C_QK_DIM // 2)

    a_scale = HEAD_DIM ** -0.5 * math.log2(math.e)
    b_scale = B_QK_DIM ** -0.5 * math.log2(math.e)
    c_scale = C_QK_DIM ** -0.5

    p = _dot(h, w_ref[:, OFF_QA:OFF_QB])
    for j in range(A_Q // LANES):
        qa_ref[0, :, j * LANES:(j + 1) * LANES] = rot(
            p[:, j * LANES:(j + 1) * LANES] * a_scale, HEAD_DIM).astype(BF16)
    ka_ref[0] = rot(p[:, A_Q:A_Q + A_KV], HEAD_DIM).astype(BF16)
    va_ref[0, :, :A_KV] = p[:, A_Q + A_KV:A_Q + 2 * A_KV].astype(BF16)
    va_ref[0, :, A_KV:] = jnp.ones((p.shape[0], A_KV), BF16)

    p = _dot(h, w_ref[:, OFF_QB:OFF_KB])
    for j in range(B_QK // LANES):
        qb_ref[0, :, j * LANES:(j + 1) * LANES] = rot(
            p[:, j * LANES:(j + 1) * LANES] * b_scale, HEAD_DIM).astype(BF16)
    p = _dot(h, w_ref[:, OFF_KB:OFF_VB])
    for j in range(B_QK // LANES):
        kb_ref[0, :, j * LANES:(j + 1) * LANES] = rot(
            p[:, j * LANES:(j + 1) * LANES], HEAD_DIM).astype(BF16)
    p = _dot(h, w_ref[:, OFF_VB:OFF_QR]).astype(BF16)
    for j in range(B_HEADS):
        vb_ref[0, :, 2 * j * B_V_DIM:(2 * j + 1) * B_V_DIM] = p[:, j * B_V_DIM:(j + 1) * B_V_DIM]
        vb_ref[0, :, (2 * j + 1) * B_V_DIM:(2 * j + 2) * B_V_DIM] = jnp.ones(
            (p.shape[0], B_V_DIM), BF16)

    p = _dot(h, w_ref[:, OFF_QR:OFF_VR])
    qr_ref[0] = rot(p[:, :C_QK], C_QK_DIM).astype(BF16)
    kr_ref[0] = rot(p[:, C_QK:] * c_scale, C_QK_DIM).astype(BF16)
    vr_ref[0] = _dot(h, w_ref[:, OFF_VR:OFF_GR]).astype(BF16)
    gr_ref[0] = _dot(h, w_ref[:, OFF_GR:IN_WIDTH])


def _in_proj(x, g, sh, sc, w, tabs, tm):
    bsz, n, d = x.shape
    rope = tabs is not None
    per_b = (lambda b, i: (b, 0, 0)) if sh.shape[0] == bsz else (lambda b, i: (0, 0, 0))
    in_specs = [
        pl.BlockSpec((1, tm, d), lambda b, i: (b, i, 0)),
        pl.BlockSpec((1, d), lambda b, i: (0, 0)),
        pl.BlockSpec((1, 1, d), per_b),
        pl.BlockSpec((1, 1, d), per_b),
        pl.BlockSpec((d, IN_WIDTH), lambda b, i: (0, 0)),
    ]
    args = [x, g.reshape(1, d), sh, sc, w]
    if rope:
        in_specs += [pl.BlockSpec((tm, LANES), lambda b, i: (i, 0))] * 4
        args += list(tabs)
    widths = (A_Q, A_KV, 2 * A_KV, B_QK, B_QK, 2 * B_V, C_QK, C_QK, C_V, C_V)
    dtypes = (BF16,) * 9 + (F32,)
    out_specs = [pl.BlockSpec((1, tm, wd), lambda b, i: (b, i, 0)) for wd in widths]
    out_shape = [jax.ShapeDtypeStruct((bsz, n, wd), dt) for wd, dt in zip(widths, dtypes)]
    return pl.pallas_call(
        functools.partial(_inproj_kernel, rope=rope),
        grid=(bsz, n // tm),
        in_specs=in_specs,
        out_specs=out_specs,
        out_shape=out_shape,
        compiler_params=_params("parallel", "parallel"),
        name="in_proj_rope" if rope else "in_proj_ctx",
    )(*args)


def _mixer_a_kernel(*refs, tq, n_lat, local):
    if local:
        sink_ref, q_ref, k_ref, v_ref, kc_ref, vc_ref, o_ref = refs
    else:
        sink_ref, q_ref, kc_ref, vc_ref, o_ref = refs
    blk = A_BLOCK
    half = HEAD_DIM
    lane = lax.broadcasted_iota(jnp.int32, (blk, LANES), 1)
    low = lane < half
    row2 = lax.broadcasted_iota(jnp.int32, (2 * blk, 1), 0)
    kc = kc_ref[0]
    vc = vc_ref[0]
    log2e = math.log2(math.e)
    if local:
        nb = n_lat // blk
        qrow = lax.broadcasted_iota(jnp.int32, (2 * blk, blk), 0) & (blk - 1)
        kcol = lax.broadcasted_iota(jnp.int32, (2 * blk, blk), 1)
        tri_prev = kcol >= qrow
        tri_next = kcol <= qrow
    for sub in range(tq // blk):
        r0 = sub * blk
        gb = pl.program_id(1) * (tq // blk) + sub
        if local:
            prev0 = pl.multiple_of(jnp.maximum(gb - 1, 0) * blk, blk)
            cur0 = pl.multiple_of(gb * blk, blk)
            next0 = pl.multiple_of(jnp.minimum(gb + 1, nb - 1) * blk, blk)
            k_all = jnp.concatenate([k_ref[0, pl.ds(prev0, blk), :], k_ref[0, pl.ds(cur0, blk), :],
                                     k_ref[0, pl.ds(next0, blk), :], kc], axis=0)
            v_all = jnp.concatenate([v_ref[0, pl.ds(prev0, blk), :], v_ref[0, pl.ds(cur0, blk), :],
                                     v_ref[0, pl.ds(next0, blk), :], vc], axis=0)
            ok_prev = jnp.logical_and(tri_prev, gb > 0)
            ok_next = jnp.logical_and(tri_next, gb < nb - 1)
        else:
            k_all, v_all = kc, vc
        for hk in range(A_KV_HEADS):
            qp = q_ref[0, r0:r0 + blk, hk * LANES:(hk + 1) * LANES].astype(F32)
            qsw = pltpu.roll(qp, half, 1)
            if hk == 0:
                q0, q1 = jnp.where(low, qp, 0.0), jnp.where(low, qsw, 0.0)
            else:
                q0, q1 = jnp.where(low, 0.0, qsw), jnp.where(low, 0.0, qp)
            qz = jnp.concatenate([q0, q1], axis=0).astype(BF16)
            sink = jnp.where(row2 < blk, sink_ref[2 * hk], sink_ref[2 * hk + 1]) * log2e
            s = _nt_dot(qz, k_all)
            if local:
                s = jnp.concatenate([jnp.where(ok_prev, s[:, :blk], NEG_INF), s[:, blk:2 * blk],
                                     jnp.where(ok_next, s[:, 2 * blk:3 * blk], NEG_INF),
                                     s[:, 3 * blk:]], axis=1)
            m = jnp.maximum(jnp.max(s, axis=-1, keepdims=True), sink)
            r = _dot(jnp.exp2(s - m).astype(BF16), v_all)
            r = r[:, :LANES] / (r[:, LANES:] + jnp.exp2(sink - m))
            ra, rb = r[:blk], r[blk:]
            if hk == 0:
                pair = jnp.where(low, ra, pltpu.roll(rb, half, 1))
            else:
                pair = jnp.where(low, pltpu.roll(ra, half, 1), rb)
            o_ref[0, r0:r0 + blk, hk * LANES:(hk + 1) * LANES] = pair.astype(BF16)


def _mixer_a(sink, q, k, v, kc, vc, tq):
    bsz, n, _ = q.shape
    ctx_len = kc.shape[1]
    local = k is not None
    in_specs = [pl.BlockSpec(memory_space=pltpu.SMEM),
                pl.BlockSpec((1, tq, A_Q), lambda b, i: (b, i, 0))]
    args = [sink, q]
    if local:
        in_specs += [pl.BlockSpec((1, n, A_KV), lambda b, i: (b, 0, 0)),
                     pl.BlockSpec((1, n, 2 * A_KV), lambda b, i: (b, 0, 0))]
        args += [k, v]
    in_specs += [pl.BlockSpec((1, ctx_len, A_KV), lambda b, i: (b, 0, 0)),
                 pl.BlockSpec((1, ctx_len, 2 * A_KV), lambda b, i: (b, 0, 0))]
    args += [kc, vc]
    return pl.pallas_call(
        functools.partial(_mixer_a_kernel, tq=tq, n_lat=n, local=local),
        grid=(bsz, n // tq),
        in_specs=in_specs,
        out_specs=pl.BlockSpec((1, tq, A_Q), lambda b, i: (b, i, 0)),
        out_shape=jax.ShapeDtypeStruct((bsz, n, A_Q), BF16),
        compiler_params=_params("parallel", "parallel"),
        name="mixer_a_window" if local else "mixer_a_ctx",
    )(*args)


def _mixer_b_kernel(*refs, tq, tk, n_lat, lam_init):
    if n_lat:
        lam_ref, g_ref, q_ref, kc_ref, vc_ref, k_ref, v_ref, o_ref, acc_ref = refs
    else:
        lam_ref, g_ref, q_ref, kc_ref, vc_ref, o_ref, acc_ref = refs
    q = q_ref[0]
    lane = lax.broadcasted_iota(jnp.int32, q.shape, 1)
    zero = jnp.zeros_like(q)
    qz = (jnp.where(lane < B_QK_DIM, q, zero), jnp.where(lane < B_QK_DIM, zero, q))

    def step(kblk, vblk, ms, first):
        out = []
        for c in range(2):
            s = _nt_dot(qz[c], kblk)
            m_new = jnp.maximum(ms[c], jnp.max(s, axis=-1, keepdims=True))
            p = jnp.exp2(s - m_new)
            pv = _dot(p.astype(BF16), vblk)
            if first:
                acc_ref[c] = pv
            else:
                acc_ref[c] = jnp.exp2(ms[c] - m_new) * acc_ref[c] + pv
            out.append(m_new)
        return tuple(out)

    m0 = jnp.full((tq, 1), NEG_INF, F32)
    ms = step(kc_ref[0], vc_ref[0], (m0, m0), True)
    for j in range(n_lat // tk):
        ms = step(k_ref[0, j * tk:(j + 1) * tk, :], v_ref[0, j * tk:(j + 1) * tk, :], ms, False)

    d = lam_ref[...]
    lam = (jnp.exp(jnp.sum(d[0:1] * d[1:2], axis=-1, keepdims=True))
           - jnp.exp(jnp.sum(d[2:3] * d[3:4], axis=-1, keepdims=True)) + lam_init)
    o = (acc_ref[0, :, :B_V_DIM] / acc_ref[0, :, B_V_DIM:]
         - lam * (acc_ref[1, :, :B_V_DIM] / acc_ref[1, :, B_V_DIM:]))
    y = _rms(o) * g_ref[...] * (1.0 - lam_init)
    o_ref[0] = y.astype(BF16)


def _mixer_b(lam_p, g, q, kc, vc, k, v, lam_init, tq, tk):
    bsz, n, _ = q.shape
    ctx_len = kc.shape[1]
    has_lat = k is not None
    n_lat = k.shape[1] if has_lat else 0
    in_specs = [
        pl.BlockSpec((4, B_QK_DIM), lambda b, h, i: (0, 0)),
        pl.BlockSpec((1, B_V_DIM), lambda b, h, i: (0, 0)),
        pl.BlockSpec((1, tq, LANES), lambda b, h, i: (b, i, h)),
        pl.BlockSpec((1, ctx_len, LANES), lambda b, h, i: (b, 0, h)),
        pl.BlockSpec((1, ctx_len, 2 * B_V_DIM), lambda b, h, i: (b, 0, h)),
    ]
    args = [lam_p, g.reshape(1, B_V_DIM), q, kc, vc]
    if has_lat:
        in_specs += [pl.BlockSpec((1, n_lat, LANES), lambda b, h, i: (b, 0, h)),
                     pl.BlockSpec((1, n_lat, 2 * B_V_DIM), lambda b, h, i: (b, 0, h))]
        args += [k, v]
    return pl.pallas_call(
        functools.partial(_mixer_b_kernel, tq=tq, tk=tk, n_lat=n_lat, lam_init=lam_init),
        grid=(bsz, B_HEADS, n // tq),
        in_specs=in_specs,
        out_specs=pl.BlockSpec((1, tq, LANES), lambda b, h, i: (b, i, h)),
        out_shape=jax.ShapeDtypeStruct((bsz, n, B_V), BF16),
        scratch_shapes=[pltpu.VMEM((2, tq, 2 * B_V_DIM), F32)],
        compiler_params=_params("parallel", "parallel", "parallel"),
        name="mixer_b_latent" if has_lat else "mixer_b_ctx",
    )(*args)


def _log_sigmoid(x):
    return jnp.minimum(x, 0.0) - jnp.log(1.0 + jnp.exp(-jnp.abs(x)))


def _per_head(ref, row, shape, axis, width):
    idx = lax.broadcasted_iota(jnp.int32, shape, axis) // width
    out = jnp.zeros(shape, F32)
    for h in range(C_HEADS):
        out = jnp.where(idx == h, ref[row, h], out)
    return out


def _mixer_c_kernel(*refs, n, ctx_len):
    if ctx_len:
        dec_ref, q_ref, k_ref, v_ref, g_ref, kc_ref, vc_ref, o_ref, sf_ref = refs
    else:
        dec_ref, q_ref, k_ref, v_ref, g_ref, o_ref, sf_ref = refs
    ch = C_CHUNK
    nc = n // ch
    qk_w, v_w = C_QK, C_V

    lgf_qk = _log_sigmoid(_per_head(dec_ref, 0, (1, qk_w), 1, C_QK_DIM))
    lgb_qk = _log_sigmoid(_per_head(dec_ref, 1, (1, qk_w), 1, C_QK_DIM))
    lgf_blk = _log_sigmoid(_per_head(dec_ref, 0, (1, C_HEADS * ch), 1, ch))
    lgb_blk = _log_sigmoid(_per_head(dec_ref, 1, (1, C_HEADS * ch), 1, ch))
    lgf_row = _log_sigmoid(_per_head(dec_ref, 0, (qk_w, 1), 0, C_QK_DIM))
    lgb_row = _log_sigmoid(_per_head(dec_ref, 1, (qk_w, 1), 0, C_QK_DIM))

    pos = lax.broadcasted_iota(jnp.int32, (ch, 1), 0).astype(F32)
    q_dec_f = jnp.exp(lgf_qk * (pos + 1.0))
    q_dec_b = jnp.exp(lgb_qk * (ch - pos))
    k_dec_f = jnp.exp(lgf_qk * (ch - 1.0 - pos))
    k_dec_b = jnp.exp(lgb_qk * pos)
    chunk_f = jnp.exp(lgf_row * float(ch))
    chunk_b = jnp.exp(lgb_row * float(ch))
    col = lax.broadcasted_iota(jnp.int32, (ch, C_HEADS * ch), 1) & (ch - 1)
    diff = (lax.broadcasted_iota(jnp.int32, (ch, C_HEADS * ch), 0) - col).astype(F32)
    decay = jnp.where(diff >= 0.0, jnp.exp(lgf_blk * jnp.maximum(diff, 0.0)),
                      jnp.exp(lgb_blk * jnp.maximum(-diff, 0.0)))
    state_mask = (lax.broadcasted_iota(jnp.int32, (qk_w, v_w), 0) // C_QK_DIM
                  == lax.broadcasted_iota(jnp.int32, (qk_w, v_w), 1) // C_V_DIM)
    head_qk = lax.broadcasted_iota(jnp.int32, (ch, qk_w), 1) // C_QK_DIM
    head_v = lax.broadcasted_iota(jnp.int32, (ch, v_w), 1) // C_V_DIM

    def kv_update(kf32, vb16, weights):
        kk = jnp.concatenate([kf32 * w for w in weights], axis=1) if len(weights) > 1 \
            else kf32 * weights[0]
        u = _dot(kk.T.astype(BF16), vb16)
        return [jnp.where(state_mask, u[j * qk_w:(j + 1) * qk_w], 0.0) for j in range(len(weights))]

    if ctx_len:
        cpos = lax.broadcasted_iota(jnp.int32, (ctx_len, 1), 0).astype(F32)
        s_f, s_b = kv_update(kc_ref[0].astype(F32), vc_ref[0],
                             [jnp.exp(lgf_qk * (ctx_len - 1.0 - cpos)), jnp.exp(lgb_qk * cpos)])
    else:
        s_f = jnp.zeros((qk_w, v_w), F32)
        s_b = jnp.zeros((qk_w, v_w), F32)

    def fwd(c, state):
        off = pl.multiple_of(c * ch, ch)
        sf_ref[c] = state.astype(BF16)
        (u_f,) = kv_update(k_ref[0, pl.ds(off, ch), :].astype(F32), v_ref[0, pl.ds(off, ch), :],
                           [k_dec_f])
        return chunk_f * state + u_f

    lax.fori_loop(0, nc, fwd, s_f, unroll=min(nc, 8))

    def bwd(t, state_b):
        c = nc - 1 - t
        off = pl.multiple_of(c * ch, ch)
        q = q_ref[0, pl.ds(off, ch), :]
        k = k_ref[0, pl.ds(off, ch), :]
        v = v_ref[0, pl.ds(off, ch), :]
        zq = jnp.zeros_like(k)
        zv = jnp.zeros_like(v)
        kz = jnp.concatenate([jnp.where(head_qk == h, k, zq) for h in range(C_HEADS)], axis=0)
        vz = jnp.concatenate([jnp.where(head_v == h, v, zv) for h in range(C_HEADS)], axis=0)
        att = (_nt_dot(q, kz) * decay).astype(BF16)
        y = _dot(att, vz)
        qf = q.astype(F32)
        qq = jnp.concatenate([qf * q_dec_f, qf * q_dec_b], axis=1).astype(BF16)
        states = jnp.concatenate([sf_ref[c], state_b.astype(BF16)], axis=0)
        y = y + _dot(qq, states)
        outs = []
        for j in range(v_w // LANES):
            yb = y[:, j * LANES:(j + 1) * LANES]
            lo = lax.broadcasted_iota(jnp.int32, yb.shape, 1) < C_V_DIM
            s_all = jnp.sum(yb, axis=-1, keepdims=True)
            s_lo = jnp.sum(jnp.where(lo, yb, 0.0), axis=-1, keepdims=True)
            yc = yb - jnp.where(lo, s_lo, s_all - s_lo) * (1.0 / C_V_DIM)
            sq = yc * yc
            q_all = jnp.sum(sq, axis=-1, keepdims=True)
            q_lo = jnp.sum(jnp.where(lo, sq, 0.0), axis=-1, keepdims=True)
            var = jnp.where(lo, q_lo, q_all - q_lo) * (1.0 / C_V_DIM)
            outs.append(yc * lax.rsqrt(var + NORM_EPS))
        yn = jnp.concatenate(outs, axis=1)
        gate = g_ref[0, pl.ds(off, ch), :]
        o_ref[0, pl.ds(off, ch), :] = (gate * _sigmoid(gate) * yn).astype(BF16)
        (u_b,) = kv_update(k.astype(F32), v, [k_dec_b])
        return chunk_b * state_b + u_b

    lax.fori_loop(0, nc, bwd, s_b, unroll=min(nc, 4))


def _mixer_c(dec, q, k, v, g, kc, vc):
    bsz, n, _ = q.shape
    ctx_len = kc.shape[1] if kc is not None else 0
    seq = lambda wd: pl.BlockSpec((1, n, wd), lambda b: (b, 0, 0))
    once = lambda wd: pl.BlockSpec((1, n, wd), lambda b: (b, 0, 0), pipeline_mode=pl.Buffered(1))
    in_specs = [pl.BlockSpec(memory_space=pltpu.SMEM), once(C_QK), once(C_QK), once(C_V), once(C_V)]
    args = [dec, q, k, v, g]
    if ctx_len:
        in_specs += [pl.BlockSpec((1, ctx_len, C_QK), lambda b: (b, 0, 0)),
                     pl.BlockSpec((1, ctx_len, C_V), lambda b: (b, 0, 0))]
        args += [kc, vc]
    return pl.pallas_call(
        functools.partial(_mixer_c_kernel, n=n, ctx_len=ctx_len),
        grid=(bsz,),
        in_specs=in_specs,
        out_specs=seq(C_V),
        out_shape=jax.ShapeDtypeStruct((bsz, n, C_V), BF16),
        scratch_shapes=[pltpu.VMEM((n // C_CHUNK, C_QK, C_V), BF16)],
        compiler_params=_params("parallel"),
        name="mixer_c_latent" if ctx_len else "mixer_c_ctx",
    )(*args)


FF_CHUNK = 256
HALO = 2 * SUBLANES


def _post_kernel(*refs, tm, d_ff, final):
    (am, ap, an, bm, bp, bn, cm, cp, cn, xm, xp, xn_, wo_ref, g1_ref, n2_ref, sh_ref, sc_ref,
     wup_ref, cw_ref, cb_ref, wdn_ref, g2_ref, fg_ref, o_ref, mix_ref, xs_ref, hs_ref, act_ref) = refs
    i = pl.program_id(1)
    last = pl.num_programs(1) - 1
    rows = tm + 2 * HALO
    col = 0
    for main, prev, nxt in ((am, ap, an), (bm, bp, bn), (cm, cp, cn)):
        wd = main.shape[-1]
        mix_ref[0:HALO, col:col + wd] = prev[0]
        mix_ref[HALO:HALO + tm, col:col + wd] = main[0]
        mix_ref[HALO + tm:rows, col:col + wd] = nxt[0]
        col += wd
    xs_ref[0:HALO, :] = xp[0]
    xs_ref[HALO:HALO + tm, :] = xm[0]
    xs_ref[HALO + tm:rows, :] = xn_[0]

    xs_ref[...] = xs_ref[...] + g1_ref[0] * _dot(mix_ref[...], wo_ref[...])
    h = (_rms(xs_ref[...]) * n2_ref[...]) * (1.0 + sc_ref[0]) + sh_ref[0]
    hs_ref[...] = h.astype(BF16)
    zero = jnp.zeros((HALO, h.shape[1]), BF16)

    @pl.when(i == 0)
    def _():
        hs_ref[0:HALO, :] = zero

    @pl.when(i == last)
    def _():
        hs_ref[HALO + tm:rows, :] = zero

    hs = hs_ref[...]

    def conv(u, off):
        w = cw_ref[:, off:off + FF_CHUNK]
        return (u[HALO - 1:HALO - 1 + tm] * w[0:1] + u[HALO:HALO + tm] * w[1:2]
                + u[HALO + 1:HALO + 1 + tm] * w[2:3] + cb_ref[:, off:off + FF_CHUNK])

    for c in range(d_ff // FF_CHUNK):
        lo = c * FF_CHUNK
        val = conv(_dot(hs, wup_ref[:, lo:lo + FF_CHUNK]), lo)
        gate = conv(_dot(hs, wup_ref[:, d_ff + lo:d_ff + lo + FF_CHUNK]), d_ff + lo)
        act_ref[:, lo:lo + FF_CHUNK] = (gate * _sigmoid(gate) * val).astype(BF16)
    y = _dot(act_ref[...], wdn_ref[...])
    out = xs_ref[HALO:HALO + tm, :] + g2_ref[0] * y
    if final:
        out = _rms(out) * fg_ref[...]
    o_ref[0] = out


def _post(a, b, c, x, w_out, g1, n2, sh, sc, w_up, conv_w, conv_b, w_down, g2, final_g, tm, final):
    bsz, n, d = x.shape
    d_ff = w_down.shape[0]
    nb = tm // HALO
    n_halo_blocks = n // HALO
    per_b = (lambda bb, i: (bb, 0, 0)) if g1.shape[0] == bsz else (lambda bb, i: (0, 0, 0))
    const = lambda shape: pl.BlockSpec(shape, lambda bb, i: (0, 0), pipeline_mode=pl.Buffered(1))
    vec = pl.BlockSpec((1, 1, d), per_b)

    def tiles(wd):
        return [pl.BlockSpec((1, tm, wd), lambda bb, i: (bb, i, 0)),
                pl.BlockSpec((1, HALO, wd), lambda bb, i: (bb, jnp.maximum(i * nb - 1, 0), 0)),
                pl.BlockSpec((1, HALO, wd),
                             lambda bb, i: (bb, jnp.minimum((i + 1) * nb, n_halo_blocks - 1), 0))]

    rows = tm + 2 * HALO
    return pl.pallas_call(
        functools.partial(_post_kernel, tm=tm, d_ff=d_ff, final=final),
        grid=(bsz, n // tm),
        in_specs=(tiles(A_Q) + tiles(B_V) + tiles(C_V) + tiles(d)
                  + [const((d, d)), vec, const((1, d)), vec, vec,
                     const((d, 2 * d_ff)), const((CONV_W, 2 * d_ff)), const((1, 2 * d_ff)),
                     const((d_ff, d)), vec, const((1, d))]),
        out_specs=pl.BlockSpec((1, tm, d), lambda bb, i: (bb, i, 0)),
        out_shape=jax.ShapeDtypeStruct((bsz, n, d), F32),
        scratch_shapes=[pltpu.VMEM((rows, d), BF16), pltpu.VMEM((rows, d), F32),
                        pltpu.VMEM((rows, d), BF16), pltpu.VMEM((tm, d_ff), BF16)],
        compiler_params=_params("parallel", "parallel"),
        name="post_final" if final else "post",
    )(a, a, a, b, b, b, c, c, c, x, x, x, w_out, g1, n2.reshape(1, d), sh, sc,
      w_up, conv_w, conv_b.reshape(1, 2 * d_ff), w_down, g2, final_g.reshape(1, d))


def _rope_tables(n_tok, dim):
    n_rows = n_tok // GRID_W
    rows = jnp.repeat(jnp.arange(n_rows, dtype=F32), GRID_W)
    cols = jnp.tile(jnp.arange(GRID_W, dtype=F32), n_rows)
    n_freq = dim // 4
    inv = ROPE_BASE ** (-jnp.arange(n_freq, dtype=F32) / n_freq)
    ang = jnp.concatenate([rows[:, None] * inv, cols[:, None] * inv], axis=-1)
    cos, sin = jnp.cos(ang), jnp.sin(ang)
    reps = LANES // dim
    cos_t = jnp.tile(jnp.concatenate([cos, cos], axis=-1), (1, reps))
    sin_t = jnp.tile(jnp.concatenate([-sin, sin], axis=-1), (1, reps))
    return cos_t, sin_t


def _tile_rows(n, target):
    t = min(n, target)
    while n % t:
        t //= 2
    return t


def kernel(x, c, ctx, c_ctx, w_mod, b_mod, norm1_g, norm2_g, w_in, w_out, attn_sink,
           diff_lambda, diff_subln_g, ret_decay_logit, w_up, conv_w, conv_b, w_down, final_g):
    bsz, n_lat, d = x.shape
    depth = w_mod.shape[0]
    tabs = _rope_tables(n_lat, HEAD_DIM) + _rope_tables(n_lat, C_QK_DIM)

    rows = -(-(bsz + 1) // SUBLANES) * SUBLANES
    cc = jnp.concatenate([c, c_ctx[None, :], jnp.zeros((rows - bsz - 1, d), F32)], axis=0)
    mod = _modulation(cc, w_mod, b_mod)

    w_in_b = [w_in[l].astype(BF16) for l in range(depth)]
    w_out_b = [w_out[l].astype(BF16) for l in range(depth)]
    w_up_b = [w_up[l].astype(BF16) for l in range(depth)]
    w_down_b = [w_down[l].astype(BF16) for l in range(depth)]

    tm_lat = _tile_rows(n_lat, 512)
    tm_ctx = _tile_rows(ctx.shape[1], 512)
    xc = ctx
    for l in range(depth):
        is_last = l == depth - 1
        lam_init = 0.8 - 0.6 * math.exp(-0.3 * l)
        m_lat = [mod[l, :bsz, j * d:(j + 1) * d][:, None, :] for j in range(N_MOD)]
        m_ctx = [mod[l, bsz:bsz + 1, j * d:(j + 1) * d][:, None, :] for j in range(N_MOD)]
        sh1, sc1, g1, sh2, sc2, g2 = m_lat
        csh1, csc1, cg1, csh2, csc2, cg2 = m_ctx

        (qa, ka, va, qb, kb, vb, qr, kr, vr, gr) = _in_proj(
            x, norm1_g[l], sh1, sc1, w_in_b[l], tabs, tm_lat)
        (qa_c, ka_c, va_c, qb_c, kb_c, vb_c, qr_c, kr_c, vr_c, gr_c) = _in_proj(
            xc, norm1_g[l], csh1, csc1, w_in_b[l], None, tm_ctx)

        a_out = _mixer_a(attn_sink[l], qa, ka, va, ka_c, va_c, tm_lat)
        b_out = _mixer_b(diff_lambda[l], diff_subln_g[l], qb, kb_c, vb_c, kb, vb,
                         lam_init, _tile_rows(n_lat, 1024), _tile_rows(n_lat, 512))
        c_out = _mixer_c(ret_decay_logit[l], qr, kr, vr, gr, kr_c, vr_c)
        x = _post(a_out, b_out, c_out, x, w_out_b[l], g1, norm2_g[l], sh2, sc2,
                  w_up_b[l], conv_w[l], conv_b[l], w_down_b[l], g2, final_g, tm_lat, is_last)

        if not is_last:
            a_c = _mixer_a(attn_sink[l], qa_c, None, None, ka_c, va_c, tm_ctx)
            b_c = _mixer_b(diff_lambda[l], diff_subln_g[l], qb_c, kb_c, vb_c, None, None,
                           lam_init, tm_ctx, tm_ctx)
            c_c = _mixer_c(ret_decay_logit[l], qr_c, kr_c, vr_c, gr_c, None, None)
            xc = _post(a_c, b_c, c_c, xc, w_out_b[l], cg1, norm2_g[l], csh2, csc2,
                       w_up_b[l], conv_w[l], conv_b[l], w_down_b[l], cg2, final_g, tm_ctx, False)
    return x
```

```python
import functools
import math

import jax
import jax.numpy as jnp
from jax import lax
from jax.experimental import pallas as pl
from jax.experimental.pallas import tpu as pltpu

F32 = jnp.float32
BF16 = jnp.bfloat16

GRID_W = 64
HEAD_DIM = 64
ROPE_BASE = 10000.0
NORM_EPS = 1e-6
NEG_INF = -1e30

A_HEADS = 4
A_KV_HEADS = 2
A_WINDOW = 128
A_BLOCK = 128
B_HEADS = 4
B_QK_DIM = 64
B_V_DIM = 128
C_HEADS = 4
C_QK_DIM = 32
C_V_DIM = 64
C_CHUNK = 128

A_Q = A_HEADS * HEAD_DIM
A_KV = A_KV_HEADS * HEAD_DIM
B_QK = B_HEADS * 2 * B_QK_DIM
B_V = B_HEADS * B_V_DIM
C_QK = C_HEADS * C_QK_DIM
C_V = C_HEADS * C_V_DIM
IN_WIDTH = A_Q + 2 * A_KV + 2 * B_QK + B_V + 2 * C_QK + 2 * C_V
N_MOD = 6
CONV_W = 3

LANES = 128
SUBLANES = 8
VMEM_LIMIT = 56 * 1024 * 1024
POST_VMEM_LIMIT = 62 * 1024 * 1024

OFF_QA = 0
OFF_QB = A_Q + 2 * A_KV
OFF_KB = OFF_QB + B_QK
OFF_VB = OFF_KB + B_QK
OFF_QR = OFF_VB + B_V
OFF_VR = OFF_QR + 2 * C_QK
OFF_GR = OFF_VR + C_V


def _params(*sem, vmem=VMEM_LIMIT):
    return pltpu.CompilerParams(dimension_semantics=sem, vmem_limit_bytes=vmem)


def _nt_dot(a, b):
    return lax.dot_general(a, b, (((1,), (1,)), ((), ())), preferred_element_type=F32)


def _dot(a, b):
    return jnp.dot(a, b, preferred_element_type=F32)


def _rms(x):
    return x * lax.rsqrt(jnp.mean(x * x, axis=-1, keepdims=True) + NORM_EPS)


def _sigmoid(x):
    return jax.nn.sigmoid(x)


def _mod_kernel(c_ref, w_ref, b_ref, o_ref):
    c = c_ref[...]
    a = (c * _sigmoid(c)).astype(BF16)
    o_ref[0] = _dot(a, w_ref[0].astype(BF16)) + b_ref[0]


def _modulation(cc, w_mod, b_mod):
    depth, d, width = w_mod.shape
    rows = cc.shape[0]
    blk = d
    return pl.pallas_call(
        _mod_kernel,
        grid=(depth, width // blk),
        in_specs=[
            pl.BlockSpec((rows, d), lambda l, j: (0, 0)),
            pl.BlockSpec((1, d, blk), lambda l, j: (l, 0, j)),
            pl.BlockSpec((1, 1, blk), lambda l, j: (l, 0, j)),
        ],
        out_specs=pl.BlockSpec((1, rows, blk), lambda l, j: (l, 0, j)),
        out_shape=jax.ShapeDtypeStruct((depth, rows, width), F32),
        compiler_params=_params("arbitrary", "arbitrary"),
        name="modulation",
    )(cc, w_mod, b_mod.reshape(depth, 1, width))


def _rope_block(blk, cos, sin_signed, half):
    lane = lax.broadcasted_iota(jnp.int32, blk.shape, 1)
    first = (lane & (2 * half - 1)) < half
    swapped = jnp.where(first, pltpu.roll(blk, LANES - half, 1), pltpu.roll(blk, half, 1))
    return blk * cos + swapped * sin_signed


def _inproj_kernel(*refs, rope):
    if rope:
        (x_ref, g_ref, sh_ref, sc_ref, w_ref, c64_ref, s64_ref, c32_ref, s32_ref,
         qa_ref, ka_ref, va_ref, qb_ref, kb_ref, vb_ref, qr_ref, kr_ref, vr_ref, gr_ref) = refs
    else:
        (x_ref, g_ref, sh_ref, sc_ref, w_ref,
         qa_ref, ka_ref, va_ref, qb_ref, kb_ref, vb_ref, qr_ref, kr_ref, vr_ref, gr_ref) = refs
    x = x_ref[0]
    h = (_rms(x) * g_ref[...]) * (1.0 + sc_ref[0]) + sh_ref[0]
    h = h.astype(BF16)

    def rot(blk, dim):
        if not rope:
            return blk
        if dim == HEAD_DIM:
            return _rope_block(blk, c64_ref[...], s64_ref[...], HEAD_DIM // 2)
        return _rope_block(blk, c32_ref[...], s32_ref[...], C_QK_DIM // 2)

    a_scale = HEAD_DIM ** -0.5 * math.log2(math.e)
    b_scale = B_QK_DIM ** -0.5 * math.log2(math.e)
    c_scale = C_QK_DIM ** -0.5

    p = _dot(h, w_ref[:, OFF_QA:OFF_QB])
    for j in range(A_Q // LANES):
        qa_ref[0, :, j * LANES:(j + 1) * LANES] = rot(
            p[:, j * LANES:(j + 1) * LANES] * a_scale, HEAD_DIM).astype(BF16)
    ka_ref[0] = rot(p[:, A_Q:A_Q + A_KV], HEAD_DIM).astype(BF16)
    va_ref[0, :, :A_KV] = p[:, A_Q + A_KV:A_Q + 2 * A_KV].astype(BF16)
    va_ref[0, :, A_KV:] = jnp.ones((p.shape[0], A_KV), BF16)

    p = _dot(h, w_ref[:, OFF_QB:OFF_KB])
    for j in range(B_QK // LANES):
        qb_ref[0, :, j * LANES:(j + 1) * LANES] = rot(
            p[:, j * LANES:(j + 1) * LANES] * b_scale, HEAD_DIM).astype(BF16)
    p = _dot(h, w_ref[:, OFF_KB:OFF_VB])
    for j in range(B_QK // LANES):
        kb_ref[0, :, j * LANES:(j + 1) * LANES] = rot(
            p[:, j * LANES:(j + 1) * LANES], HEAD_DIM).astype(BF16)
    p = _dot(h, w_ref[:, OFF_VB:OFF_QR]).astype(BF16)
    for j in range(B_HEADS):
        vb_ref[0, :, 2 * j * B_V_DIM:(2 * j + 1) * B_V_DIM] = p[:, j * B_V_DIM:(j + 1) * B_V_DIM]
        vb_ref[0, :, (2 * j + 1) * B_V_DIM:(2 * j + 2) * B_V_DIM] = jnp.ones(
            (p.shape[0], B_V_DIM), BF16)

    p = _dot(h, w_ref[:, OFF_QR:OFF_VR])
    qr_ref[0] = rot(p[:, :C_QK], C_QK_DIM).astype(BF16)
    kr_ref[0] = rot(p[:, C_QK:] * c_scale, C_QK_DIM).astype(BF16)
    vr_ref[0] = _dot(h, w_ref[:, OFF_VR:OFF_GR]).astype(BF16)
    gr_ref[0] = _dot(h, w_ref[:, OFF_GR:IN_WIDTH])


def _in_proj(x, g, sh, sc, w, tabs, tm):
    bsz, n, d = x.shape
    rope = tabs is not None
    per_b = (lambda b, i: (b, 0, 0)) if sh.shape[0] == bsz else (lambda b, i: (0, 0, 0))
    in_specs = [
        pl.BlockSpec((1, tm, d), lambda b, i: (b, i, 0)),
        pl.BlockSpec((1, d), lambda b, i: (0, 0)),
        pl.BlockSpec((1, 1, d), per_b),
        pl.BlockSpec((1, 1, d), per_b),
        pl.BlockSpec((d, IN_WIDTH), lambda b, i: (0, 0)),
    ]
    args = [x, g.reshape(1, d), sh, sc, w]
    if rope:
        in_specs += [pl.BlockSpec((tm, LANES), lambda b, i: (i, 0))] * 4
        args += list(tabs)
    widths = (A_Q, A_KV, 2 * A_KV, B_QK, B_QK, 2 * B_V, C_QK, C_QK, C_V, C_V)
    dtypes = (BF16,) * 9 + (F32,)
    out_specs = [pl.BlockSpec((1, tm, wd), lambda b, i: (b, i, 0)) for wd in widths]
    out_shape = [jax.ShapeDtypeStruct((bsz, n, wd), dt) for wd, dt in zip(widths, dtypes)]
    return pl.pallas_call(
        functools.partial(_inproj_kernel, rope=rope),
        grid=(bsz, n // tm),
        in_specs=in_specs,
        out_specs=out_specs,
        out_shape=out_shape,
        compiler_params=_params("parallel", "parallel"),
        name="in_proj_rope" if rope else "in_proj_ctx",
    )(*args)


def _mixer_a_kernel(*refs, tq, n_lat, local):
    if local:
        sink_ref, q_ref, k_ref, v_ref, kc_ref, vc_ref, o_ref = refs
    else:
        sink_ref, q_ref, kc_ref, vc_ref, o_ref = refs
    blk = A_BLOCK
    half = HEAD_DIM
    lane = lax.broadcasted_iota(jnp.int32, (blk, LANES), 1)
    low = lane < half
    rows = A_HEADS * blk
    row_head = lax.broadcasted_iota(jnp.int32, (rows, 1), 0) // blk
    sink = jnp.zeros((rows, 1), F32)
    for h in range(A_HEADS):
        sink = jnp.where(row_head == h, sink_ref[h], sink)
    sink = sink * math.log2(math.e)
    kc = kc_ref[0]
    vc = vc_ref[0]
    if local:
        nb = n_lat // blk
        qrow = lax.broadcasted_iota(jnp.int32, (rows, blk), 0) & (blk - 1)
        kcol = lax.broadcasted_iota(jnp.int32, (rows, blk), 1)
        tri_prev = kcol >= qrow
        tri_next = kcol <= qrow
    for sub in range(tq // blk):
        r0 = sub * blk
        gb = pl.program_id(1) * (tq // blk) + sub
        if local:
            prev0 = pl.multiple_of(jnp.maximum(gb - 1, 0) * blk, blk)
            cur0 = pl.multiple_of(gb * blk, blk)
            next0 = pl.multiple_of(jnp.minimum(gb + 1, nb - 1) * blk, blk)
            k_all = jnp.concatenate([k_ref[0, pl.ds(prev0, blk), :], k_ref[0, pl.ds(cur0, blk), :],
                                     k_ref[0, pl.ds(next0, blk), :], kc], axis=0)
            v_all = jnp.concatenate([v_ref[0, pl.ds(prev0, blk), :], v_ref[0, pl.ds(cur0, blk), :],
                                     v_ref[0, pl.ds(next0, blk), :], vc], axis=0)
            ok_prev = jnp.logical_and(tri_prev, gb > 0)
            ok_next = jnp.logical_and(tri_next, gb < nb - 1)
        else:
            k_all, v_all = kc, vc
        qs = []
        for hk in range(A_KV_HEADS):
            qp = q_ref[0, r0:r0 + blk, hk * LANES:(hk + 1) * LANES].astype(F32)
            qsw = pltpu.roll(qp, half, 1)
            if hk == 0:
                qs += [jnp.where(low, qp, 0.0), jnp.where(low, qsw, 0.0)]
            else:
                qs += [jnp.where(low, 0.0, qsw), jnp.where(low, 0.0, qp)]
        qz = jnp.concatenate(qs, axis=0).astype(BF16)
        s = _nt_dot(qz, k_all)
        if local:
            s = jnp.concatenate([jnp.where(ok_prev, s[:, :blk], NEG_INF), s[:, blk:2 * blk],
                                 jnp.where(ok_next, s[:, 2 * blk:3 * blk], NEG_INF),
                                 s[:, 3 * blk:]], axis=1)
        m = jnp.maximum(jnp.max(s, axis=-1, keepdims=True), sink)
        r = _dot(jnp.exp2(s - m).astype(BF16), v_all)
        r = r[:, :LANES] / (r[:, LANES:] + jnp.exp2(sink - m))
        o_ref[0, r0:r0 + blk, :LANES] = jnp.where(
            low, r[:blk], pltpu.roll(r[blk:2 * blk], half, 1)).astype(BF16)
        o_ref[0, r0:r0 + blk, LANES:] = jnp.where(
            low, pltpu.roll(r[2 * blk:3 * blk], half, 1), r[3 * blk:]).astype(BF16)


def _mixer_a(sink, q, k, v, kc, vc, tq):
    bsz, n, _ = q.shape
    ctx_len = kc.shape[1]
    local = k is not None
    in_specs = [pl.BlockSpec(memory_space=pltpu.SMEM),
                pl.BlockSpec((1, tq, A_Q), lambda b, i: (b, i, 0))]
    args = [sink, q]
    if local:
        in_specs += [pl.BlockSpec((1, n, A_KV), lambda b, i: (b, 0, 0)),
                     pl.BlockSpec((1, n, 2 * A_KV), lambda b, i: (b, 0, 0))]
        args += [k, v]
    in_specs += [pl.BlockSpec((1, ctx_len, A_KV), lambda b, i: (b, 0, 0)),
                 pl.BlockSpec((1, ctx_len, 2 * A_KV), lambda b, i: (b, 0, 0))]
    args += [kc, vc]
    return pl.pallas_call(
        functools.partial(_mixer_a_kernel, tq=tq, n_lat=n, local=local),
        grid=(bsz, n // tq),
        in_specs=in_specs,
        out_specs=pl.BlockSpec((1, tq, A_Q), lambda b, i: (b, i, 0)),
        out_shape=jax.ShapeDtypeStruct((bsz, n, A_Q), BF16),
        compiler_params=_params("parallel", "parallel"),
        name="mixer_a_window" if local else "mixer_a_ctx",
    )(*args)


def _mixer_b_kernel(*refs, tq, tk, n_lat, lam_init):
    if n_lat:
        lam_ref, g_ref, q_ref, kc_ref, vc_ref, k_ref, v_ref, o_ref, acc_ref = refs
    else:
        lam_ref, g_ref, q_ref, kc_ref, vc_ref, o_ref, acc_ref = refs
    q = q_ref[0]
    lane = lax.broadcasted_iota(jnp.int32, q.shape, 1)
    zero = jnp.zeros_like(q)
    qz = (jnp.where(lane < B_QK_DIM, q, zero), jnp.where(lane < B_QK_DIM, zero, q))

    def step(kblk, vblk, ms, first):
        out = []
        for c in range(2):
            s = _nt_dot(qz[c], kblk)
            m_new = jnp.maximum(ms[c], jnp.max(s, axis=-1, keepdims=True))
            p = jnp.exp2(s - m_new)
            pv = _dot(p.astype(BF16), vblk)
            if first:
                acc_ref[c] = pv
            else:
                acc_ref[c] = jnp.exp2(ms[c] - m_new) * acc_ref[c] + pv
            out.append(m_new)
        return tuple(out)

    m0 = jnp.full((tq, 1), NEG_INF, F32)
    ms = step(kc_ref[0], vc_ref[0], (m0, m0), True)
    for j in range(n_lat // tk):
        ms = step(k_ref[0, j * tk:(j + 1) * tk, :], v_ref[0, j * tk:(j + 1) * tk, :], ms, False)

    d = lam_ref[...]
    lam = (jnp.exp(jnp.sum(d[0:1] * d[1:2], axis=-1, keepdims=True))
           - jnp.exp(jnp.sum(d[2:3] * d[3:4], axis=-1, keepdims=True)) + lam_init)
    o = (acc_ref[0, :, :B_V_DIM] / acc_ref[0, :, B_V_DIM:]
         - lam * (acc_ref[1, :, :B_V_DIM] / acc_ref[1, :, B_V_DIM:]))
    y = _rms(o) * g_ref[...] * (1.0 - lam_init)
    o_ref[0] = y.astype(BF16)


def _mixer_b(lam_p, g, q, kc, vc, k, v, lam_init, tq, tk):
    bsz, n, _ = q.shape
    ctx_len = kc.shape[1]
    has_lat = k is not None
    n_lat = k.shape[1] if has_lat else 0
    in_specs = [
        pl.BlockSpec((4, B_QK_DIM), lambda b, h, i: (0, 0)),
        pl.BlockSpec((1, B_V_DIM), lambda b, h, i: (0, 0)),
        pl.BlockSpec((1, tq, LANES), lambda b, h, i: (b, i, h)),
        pl.BlockSpec((1, ctx_len, LANES), lambda b, h, i: (b, 0, h)),
        pl.BlockSpec((1, ctx_len, 2 * B_V_DIM), lambda b, h, i: (b, 0, h)),
    ]
    args = [lam_p, g.reshape(1, B_V_DIM), q, kc, vc]
    if has_lat:
        in_specs += [pl.BlockSpec((1, n_lat, LANES), lambda b, h, i: (b, 0, h)),
                     pl.BlockSpec((1, n_lat, 2 * B_V_DIM), lambda b, h, i: (b, 0, h))]
        args += [k, v]
    return pl.pallas_call(
        functools.partial(_mixer_b_kernel, tq=tq, tk=tk, n_lat=n_lat, lam_init=lam_init),
        grid=(bsz, B_HEADS, n // tq),
        in_specs=in_specs,
        out_specs=pl.BlockSpec((1, tq, LANES), lambda b, h, i: (b, i, h)),
        out_shape=jax.ShapeDtypeStruct((bsz, n, B_V), BF16),
        scratch_shapes=[pltpu.VMEM((2, tq, 2 * B_V_DIM), F32)],
        compiler_params=_params("parallel", "parallel", "parallel"),
        name="mixer_b_latent" if has_lat else "mixer_b_ctx",
    )(*args)


def _log_sigmoid(x):
    return jnp.minimum(x, 0.0) - jnp.log(1.0 + jnp.exp(-jnp.abs(x)))


def _per_head(ref, row, shape, axis, width):
    idx = lax.broadcasted_iota(jnp.int32, shape, axis) // width
    out = jnp.zeros(shape, F32)
    for h in range(C_HEADS):
        out = jnp.where(idx == h, ref[row, h], out)
    return out


def _mixer_c_kernel(*refs, n, ctx_len):
    if ctx_len:
        dec_ref, q_ref, k_ref, v_ref, g_ref, kc_ref, vc_ref, o_ref, sf_ref = refs
    else:
        dec_ref, q_ref, k_ref, v_ref, g_ref, o_ref, sf_ref = refs
    ch = C_CHUNK
    nc = n // ch
    qk_w, v_w = C_QK, C_V

    lgf_qk = _log_sigmoid(_per_head(dec_ref, 0, (1, qk_w), 1, C_QK_DIM))
    lgb_qk = _log_sigmoid(_per_head(dec_ref, 1, (1, qk_w), 1, C_QK_DIM))
    lgf_blk = _log_sigmoid(_per_head(dec_ref, 0, (1, C_HEADS * ch), 1, ch))
    lgb_blk = _log_sigmoid(_per_head(dec_ref, 1, (1, C_HEADS * ch), 1, ch))
    lgf_row = _log_sigmoid(_per_head(dec_ref, 0, (qk_w, 1), 0, C_QK_DIM))
    lgb_row = _log_sigmoid(_per_head(dec_ref, 1, (qk_w, 1), 0, C_QK_DIM))

    pos = lax.broadcasted_iota(jnp.int32, (ch, 1), 0).astype(F32)
    q_dec_f = jnp.exp(lgf_qk * (pos + 1.0))
    q_dec_b = jnp.exp(lgb_qk * (ch - pos))
    k_dec_f = jnp.exp(lgf_qk * (ch - 1.0 - pos))
    k_dec_b = jnp.exp(lgb_qk * pos)
    chunk_f = jnp.exp(lgf_row * float(ch))
    chunk_b = jnp.exp(lgb_row * float(ch))
    col = lax.broadcasted_iota(jnp.int32, (ch, C_HEADS * ch), 1) & (ch - 1)
    diff = (lax.broadcasted_iota(jnp.int32, (ch, C_HEADS * ch), 0) - col).astype(F32)
    decay = jnp.where(diff >= 0.0, jnp.exp(lgf_blk * jnp.maximum(diff, 0.0)),
                      jnp.exp(lgb_blk * jnp.maximum(-diff, 0.0)))
    state_mask = (lax.broadcasted_iota(jnp.int32, (qk_w, v_w), 0) // C_QK_DIM
                  == lax.broadcasted_iota(jnp.int32, (qk_w, v_w), 1) // C_V_DIM)
    head_qk = lax.broadcasted_iota(jnp.int32, (ch, qk_w), 1) // C_QK_DIM
    head_v = lax.broadcasted_iota(jnp.int32, (ch, v_w), 1) // C_V_DIM

    def kv_update(kf32, vb16, weights):
        kk = jnp.concatenate([kf32 * w for w in weights], axis=1) if len(weights) > 1 \
            else kf32 * weights[0]
        u = _dot(kk.T.astype(BF16), vb16)
        return [jnp.where(state_mask, u[j * qk_w:(j + 1) * qk_w], 0.0) for j in range(len(weights))]

    if ctx_len:
        cpos = lax.broadcasted_iota(jnp.int32, (ctx_len, 1), 0).astype(F32)
        s_f, s_b = kv_update(kc_ref[0].astype(F32), vc_ref[0],
                             [jnp.exp(lgf_qk * (ctx_len - 1.0 - cpos)), jnp.exp(lgb_qk * cpos)])
    else:
        s_f = jnp.zeros((qk_w, v_w), F32)
        s_b = jnp.zeros((qk_w, v_w), F32)

    def fwd(c, state):
        off = pl.multiple_of(c * ch, ch)
        sf_ref[c] = state.astype(BF16)
        (u_f,) = kv_update(k_ref[0, pl.ds(off, ch), :].astype(F32), v_ref[0, pl.ds(off, ch), :],
                           [k_dec_f])
        return chunk_f * state + u_f

    lax.fori_loop(0, nc, fwd, s_f, unroll=min(nc, 8))

    def bwd(t, state_b):
        c = nc - 1 - t
        off = pl.multiple_of(c * ch, ch)
        q = q_ref[0, pl.ds(off, ch), :]
        k = k_ref[0, pl.ds(off, ch), :]
        v = v_ref[0, pl.ds(off, ch), :]
        zq = jnp.zeros_like(k)
        zv = jnp.zeros_like(v)
        kz = jnp.concatenate([jnp.where(head_qk == h, k, zq) for h in range(C_HEADS)], axis=0)
        vz = jnp.concatenate([jnp.where(head_v == h, v, zv) for h in range(C_HEADS)], axis=0)
        att = (_nt_dot(q, kz) * decay).astype(BF16)
        y = _dot(att, vz)
        qf = q.astype(F32)
        qq = jnp.concatenate([qf * q_dec_f, qf * q_dec_b], axis=1).astype(BF16)
        states = jnp.concatenate([sf_ref[c], state_b.astype(BF16)], axis=0)
        y = y + _dot(qq, states)
        outs = []
        for j in range(v_w // LANES):
            yb = y[:, j * LANES:(j + 1) * LANES]
            lo = lax.broadcasted_iota(jnp.int32, yb.shape, 1) < C_V_DIM
            s_all = jnp.sum(yb, axis=-1, keepdims=True)
            s_lo = jnp.sum(jnp.where(lo, yb, 0.0), axis=-1, keepdims=True)
            yc = yb - jnp.where(lo, s_lo, s_all - s_lo) * (1.0 / C_V_DIM)
            sq = yc * yc
            q_all = jnp.sum(sq, axis=-1, keepdims=True)
            q_lo = jnp.sum(jnp.where(lo, sq, 0.0), axis=-1, keepdims=True)
            var = jnp.where(lo, q_lo, q_all - q_lo) * (1.0 / C_V_DIM)
            outs.append(yc * lax.rsqrt(var + NORM_EPS))
        yn = jnp.concatenate(outs, axis=1)
        gate = g_ref[0, pl.ds(off, ch), :]
        o_ref[0, pl.ds(off, ch), :] = (gate * _sigmoid(gate) * yn).astype(BF16)
        (u_b,) = kv_update(k.astype(F32), v, [k_dec_b])
        return chunk_b * state_b + u_b

    lax.fori_loop(0, nc, bwd, s_b, unroll=min(nc, 4))


def _mixer_c(dec, q, k, v, g, kc, vc):
    bsz, n, _ = q.shape
    ctx_len = kc.shape[1] if kc is not None else 0
    seq = lambda wd: pl.BlockSpec((1, n, wd), lambda b: (b, 0, 0))
    once = lambda wd: pl.BlockSpec((1, n, wd), lambda b: (b, 0, 0), pipeline_mode=pl.Buffered(1))
    in_specs = [pl.BlockSpec(memory_space=pltpu.SMEM), once(C_QK), once(C_QK), once(C_V), once(C_V)]
    args = [dec, q, k, v, g]
    if ctx_len:
        in_specs += [pl.BlockSpec((1, ctx_len, C_QK), lambda b: (b, 0, 0)),
                     pl.BlockSpec((1, ctx_len, C_V), lambda b: (b, 0, 0))]
        args += [kc, vc]
    return pl.pallas_call(
        functools.partial(_mixer_c_kernel, n=n, ctx_len=ctx_len),
        grid=(bsz,),
        in_specs=in_specs,
        out_specs=seq(C_V),
        out_shape=jax.ShapeDtypeStruct((bsz, n, C_V), BF16),
        scratch_shapes=[pltpu.VMEM((n // C_CHUNK, C_QK, C_V), BF16)],
        compiler_params=_params("parallel"),
        name="mixer_c_latent" if ctx_len else "mixer_c_ctx",
    )(*args)


FF_CHUNK = 256
HALO = 2 * SUBLANES


def _post_kernel(*refs, tm, d_ff, final):
    (am, ap, an, bm, bp, bn, cm, cp, cn, xm, xp, xn_, wo_ref, g1_ref, n2_ref, sh_ref, sc_ref,
     wup_ref, cw_ref, cb_ref, wdn_ref, g2_ref, fg_ref, o_ref, mix_ref, xs_ref, hs_ref, act_ref) = refs
    i = pl.program_id(1)
    last = pl.num_programs(1) - 1
    rows = tm + 2 * HALO
    col = 0
    for main, prev, nxt in ((am, ap, an), (bm, bp, bn), (cm, cp, cn)):
        wd = main.shape[-1]
        mix_ref[0:HALO, col:col + wd] = prev[0]
        mix_ref[HALO:HALO + tm, col:col + wd] = main[0]
        mix_ref[HALO + tm:rows, col:col + wd] = nxt[0]
        col += wd
    xs_ref[0:HALO, :] = xp[0]
    xs_ref[HALO:HALO + tm, :] = xm[0]
    xs_ref[HALO + tm:rows, :] = xn_[0]

    xs_ref[...] = xs_ref[...] + g1_ref[0] * _dot(mix_ref[...], wo_ref[...])
    h = (_rms(xs_ref[...]) * n2_ref[...]) * (1.0 + sc_ref[0]) + sh_ref[0]
    hs_ref[...] = h.astype(BF16)
    zero = jnp.zeros((HALO, h.shape[1]), BF16)

    @pl.when(i == 0)
    def _():
        hs_ref[0:HALO, :] = zero

    @pl.when(i == last)
    def _():
        hs_ref[HALO + tm:rows, :] = zero

    hs = hs_ref[...]

    def conv(u, off):
        w = cw_ref[:, off:off + FF_CHUNK]
        return (u[HALO - 1:HALO - 1 + tm] * w[0:1] + u[HALO:HALO + tm] * w[1:2]
                + u[HALO + 1:HALO + 1 + tm] * w[2:3] + cb_ref[:, off:off + FF_CHUNK])

    for c in range(d_ff // FF_CHUNK):
        lo = c * FF_CHUNK
        val = conv(_dot(hs, wup_ref[:, lo:lo + FF_CHUNK]), lo)
        gate = conv(_dot(hs, wup_ref[:, d_ff + lo:d_ff + lo + FF_CHUNK]), d_ff + lo)
        act_ref[:, lo:lo + FF_CHUNK] = (gate * _sigmoid(gate) * val).astype(BF16)
    y = _dot(act_ref[...], wdn_ref[...])
    out = xs_ref[HALO:HALO + tm, :] + g2_ref[0] * y
    if final:
        out = _rms(out) * fg_ref[...]
    o_ref[0] = out


def _post(a, b, c, x, w_out, g1, n2, sh, sc, w_up, conv_w, conv_b, w_down, g2, final_g, tm, final):
    bsz, n, d = x.shape
    d_ff = w_down.shape[0]
    nb = tm // HALO
    n_halo_blocks = n // HALO
    per_b = (lambda bb, i: (bb, 0, 0)) if g1.shape[0] == bsz else (lambda bb, i: (0, 0, 0))
    const = lambda shape: pl.BlockSpec(shape, lambda bb, i: (0, 0), pipeline_mode=pl.Buffered(1))
    vec = pl.BlockSpec((1, 1, d), per_b)

    def tiles(wd):
        return [pl.BlockSpec((1, tm, wd), lambda bb, i: (bb, i, 0)),
                pl.BlockSpec((1, HALO, wd), lambda bb, i: (bb, jnp.maximum(i * nb - 1, 0), 0)),
                pl.BlockSpec((1, HALO, wd),
                             lambda bb, i: (bb, jnp.minimum((i + 1) * nb, n_halo_blocks - 1), 0))]

    rows = tm + 2 * HALO
    return pl.pallas_call(
        functools.partial(_post_kernel, tm=tm, d_ff=d_ff, final=final),
        grid=(bsz, n // tm),
        in_specs=(tiles(A_Q) + tiles(B_V) + tiles(C_V) + tiles(d)
                  + [const((d, d)), vec, const((1, d)), vec, vec,
                     const((d, 2 * d_ff)), const((CONV_W, 2 * d_ff)), const((1, 2 * d_ff)),
                     const((d_ff, d)), vec, const((1, d))]),
        out_specs=pl.BlockSpec((1, tm, d), lambda bb, i: (bb, i, 0)),
        out_shape=jax.ShapeDtypeStruct((bsz, n, d), F32),
        scratch_shapes=[pltpu.VMEM((rows, d), BF16), pltpu.VMEM((rows, d), F32),
                        pltpu.VMEM((rows, d), BF16), pltpu.VMEM((tm, d_ff), BF16)],
        compiler_params=_params("parallel", "parallel", vmem=POST_VMEM_LIMIT),
        name="post_final" if final else "post",
    )(a, a, a, b, b, b, c, c, c, x, x, x, w_out, g1, n2.reshape(1, d), sh, sc,
      w_up, conv_w, conv_b.reshape(1, 2 * d_ff), w_down, g2, final_g.reshape(1, d))


def _rope_tables(n_tok, dim):
    n_rows = n_tok // GRID_W
    rows = jnp.repeat(jnp.arange(n_rows, dtype=F32), GRID_W)
    cols = jnp.tile(jnp.arange(GRID_W, dtype=F32), n_rows)
    n_freq = dim // 4
    inv = ROPE_BASE ** (-jnp.arange(n_freq, dtype=F32) / n_freq)
    ang = jnp.concatenate([rows[:, None] * inv, cols[:, None] * inv], axis=-1)
    cos, sin = jnp.cos(ang), jnp.sin(ang)
    reps = LANES // dim
    cos_t = jnp.tile(jnp.concatenate([cos, cos], axis=-1), (1, reps))
    sin_t = jnp.tile(jnp.concatenate([-sin, sin], axis=-1), (1, reps))
    return cos_t, sin_t


def _tile_rows(n, target):
    t = min(n, target)
    while n % t:
        t //= 2
    return t


def kernel(x, c, ctx, c_ctx, w_mod, b_mod, norm1_g, norm2_g, w_in, w_out, attn_sink,
           diff_lambda, diff_subln_g, ret_decay_logit, w_up, conv_w, conv_b, w_down, final_g):
    bsz, n_lat, d = x.shape
    depth = w_mod.shape[0]
    tabs = _rope_tables(n_lat, HEAD_DIM) + _rope_tables(n_lat, C_QK_DIM)

    rows = -(-(bsz + 1) // SUBLANES) * SUBLANES
    cc = jnp.concatenate([c, c_ctx[None, :], jnp.zeros((rows - bsz - 1, d), F32)], axis=0)
    mod = _modulation(cc, w_mod, b_mod)

    w_in_b = [w_in[l].astype(BF16) for l in range(depth)]
    w_out_b = [w_out[l].astype(BF16) for l in range(depth)]
    w_up_b = [w_up[l].astype(BF16) for l in range(depth)]
    w_down_b = [w_down[l].astype(BF16) for l in range(depth)]

    tm_lat = _tile_rows(n_lat, 512)
    tm_ctx = _tile_rows(ctx.shape[1], 512)
    xc = ctx
    for l in range(depth):
        is_last = l == depth - 1
        lam_init = 0.8 - 0.6 * math.exp(-0.3 * l)
        m_lat = [mod[l, :bsz, j * d:(j + 1) * d][:, None, :] for j in range(N_MOD)]
        m_ctx = [mod[l, bsz:bsz + 1, j * d:(j + 1) * d][:, None, :] for j in range(N_MOD)]
        sh1, sc1, g1, sh2, sc2, g2 = m_lat
        csh1, csc1, cg1, csh2, csc2, cg2 = m_ctx

        (qa, ka, va, qb, kb, vb, qr, kr, vr, gr) = _in_proj(
            x, norm1_g[l], sh1, sc1, w_in_b[l], tabs, tm_lat)
        (qa_c, ka_c, va_c, qb_c, kb_c, vb_c, qr_c, kr_c, vr_c, gr_c) = _in_proj(
            xc, norm1_g[l], csh1, csc1, w_in_b[l], None, tm_ctx)

        a_out = _mixer_a(attn_sink[l], qa, ka, va, ka_c, va_c, tm_lat)
        b_out = _mixer_b(diff_lambda[l], diff_subln_g[l], qb, kb_c, vb_c, kb, vb,
                         lam_init, _tile_rows(n_lat, 1024), _tile_rows(n_lat, 512))
        c_out = _mixer_c(ret_decay_logit[l], qr, kr, vr, gr, kr_c, vr_c)
        x = _post(a_out, b_out, c_out, x, w_out_b[l], g1, norm2_g[l], sh2, sc2,
                  w_up_b[l], conv_w[l], conv_b[l], w_down_b[l], g2, final_g,
                  _tile_rows(n_lat, 1024), is_last)

        if not is_last:
            a_c = _mixer_a(attn_sink[l], qa_c, None, None, ka_c, va_c, tm_ctx)
            b_c = _mixer_b(diff_lambda[l], diff_subln_g[l], qb_c, kb_c, vb_c, None, None,
                           lam_init, tm_ctx, tm_ctx)
            c_c = _mixer_c(ret_decay_logit[l], qr_c, kr_c, vr_c, gr_c, None, None)
            xc = _post(a_c, b_c, c_c, xc, w_out_b[l], cg1, norm2_g[l], csh2, csc2,
                       w_up_b[l], conv_w[l], conv_b[l], w_down_b[l], cg2, final_g, tm_ctx, False)
    return x
```

```python
import functools
import math

import jax
import jax.numpy as jnp
from jax import lax
from jax.experimental import pallas as pl
from jax.experimental.pallas import tpu as pltpu

F32 = jnp.float32
BF16 = jnp.bfloat16

GRID_W = 64
HEAD_DIM = 64
ROPE_BASE = 10000.0
NORM_EPS = 1e-6
NEG_INF = -1e30

A_HEADS = 4
A_KV_HEADS = 2
A_WINDOW = 128
A_BLOCK = 128
B_HEADS = 4
B_QK_DIM = 64
B_V_DIM = 128
C_HEADS = 4
C_QK_DIM = 32
C_V_DIM = 64
C_CHUNK = 128

A_Q = A_HEADS * HEAD_DIM
A_KV = A_KV_HEADS * HEAD_DIM
B_QK = B_HEADS * 2 * B_QK_DIM
B_V = B_HEADS * B_V_DIM
C_QK = C_HEADS * C_QK_DIM
C_V = C_HEADS * C_V_DIM
IN_WIDTH = A_Q + 2 * A_KV + 2 * B_QK + B_V + 2 * C_QK + 2 * C_V
N_MOD = 6
CONV_W = 3

LANES = 128
SUBLANES = 8
VMEM_LIMIT = 56 * 1024 * 1024
POST_VMEM_LIMIT = 62 * 1024 * 1024

OFF_QA = 0
OFF_QB = A_Q + 2 * A_KV
OFF_KB = OFF_QB + B_QK
OFF_VB = OFF_KB + B_QK
OFF_QR = OFF_VB + B_V
OFF_VR = OFF_QR + 2 * C_QK
OFF_GR = OFF_VR + C_V


def _params(*sem, vmem=VMEM_LIMIT):
    return pltpu.CompilerParams(dimension_semantics=sem, vmem_limit_bytes=vmem)


def _layer_spec(arr, l, single_buffer=False):
    zeros = (0,) * (arr.ndim - 1)
    mode = dict(pipeline_mode=pl.Buffered(1)) if single_buffer else {}
    return pl.BlockSpec((None,) + arr.shape[1:], lambda *_: (l,) + zeros, **mode)


def _nt_dot(a, b):
    return lax.dot_general(a, b, (((1,), (1,)), ((), ())), preferred_element_type=F32)


def _dot(a, b):
    return jnp.dot(a, b, preferred_element_type=F32)


def _rms(x):
    return x * lax.rsqrt(jnp.mean(x * x, axis=-1, keepdims=True) + NORM_EPS)


def _sigmoid(x):
    return jax.nn.sigmoid(x)


def _mod_kernel(c_ref, w_ref, b_ref, o_ref):
    c = c_ref[...]
    a = (c * _sigmoid(c)).astype(BF16)
    o_ref[0] = _dot(a, w_ref[0].astype(BF16)) + b_ref[0]


def _modulation(cc, w_mod, b_mod):
    depth, d, width = w_mod.shape
    rows = cc.shape[0]
    blk = d
    return pl.pallas_call(
        _mod_kernel,
        grid=(depth, width // blk),
        in_specs=[
            pl.BlockSpec((rows, d), lambda l, j: (0, 0)),
            pl.BlockSpec((1, d, blk), lambda l, j: (l, 0, j)),
            pl.BlockSpec((1, 1, blk), lambda l, j: (l, 0, j)),
        ],
        out_specs=pl.BlockSpec((1, rows, blk), lambda l, j: (l, 0, j)),
        out_shape=jax.ShapeDtypeStruct((depth, rows, width), F32),
        compiler_params=_params("arbitrary", "arbitrary"),
        name="modulation",
    )(cc, w_mod, b_mod.reshape(depth, 1, width))


def _rope_block(blk, cos, sin_signed, half):
    lane = lax.broadcasted_iota(jnp.int32, blk.shape, 1)
    first = (lane & (2 * half - 1)) < half
    swapped = jnp.where(first, pltpu.roll(blk, LANES - half, 1), pltpu.roll(blk, half, 1))
    return blk * cos + swapped * sin_signed


def _inproj_kernel(*refs, rope):
    if rope:
        (x_ref, g_ref, sh_ref, sc_ref, w_ref, c64_ref, s64_ref, c32_ref, s32_ref,
         qa_ref, ka_ref, va_ref, qb_ref, kb_ref, vb_ref, qr_ref, kr_ref, vr_ref, gr_ref) = refs
    else:
        (x_ref, g_ref, sh_ref, sc_ref, w_ref,
         qa_ref, ka_ref, va_ref, qb_ref, kb_ref, vb_ref, qr_ref, kr_ref, vr_ref, gr_ref) = refs
    x = x_ref[0]
    h = (_rms(x) * g_ref[...]) * (1.0 + sc_ref[0]) + sh_ref[0]
    h = h.astype(BF16)

    def rot(blk, dim):
        if not rope:
            return blk
        if dim == HEAD_DIM:
            return _rope_block(blk, c64_ref[...], s64_ref[...], HEAD_DIM // 2)
        return _rope_block(blk, c32_ref[...], s32_ref[...], C_QK_DIM // 2)

    a_scale = HEAD_DIM ** -0.5 * math.log2(math.e)
    b_scale = B_QK_DIM ** -0.5 * math.log2(math.e)
    c_scale = C_QK_DIM ** -0.5

    p = _dot(h, w_ref[:, OFF_QA:OFF_QB])
    for j in range(A_Q // LANES):
        qa_ref[0, :, j * LANES:(j + 1) * LANES] = rot(
            p[:, j * LANES:(j + 1) * LANES] * a_scale, HEAD_DIM).astype(BF16)
    ka_ref[0] = rot(p[:, A_Q:A_Q + A_KV], HEAD_DIM).astype(BF16)
    va_ref[0, :, :A_KV] = p[:, A_Q + A_KV:A_Q + 2 * A_KV].astype(BF16)
    va_ref[0, :, A_KV:] = jnp.ones((p.shape[0], A_KV), BF16)

    p = _dot(h, w_ref[:, OFF_QB:OFF_KB])
    for j in range(B_QK // LANES):
        qb_ref[0, :, j * LANES:(j + 1) * LANES] = rot(
            p[:, j * LANES:(j + 1) * LANES] * b_scale, HEAD_DIM).astype(BF16)
    p = _dot(h, w_ref[:, OFF_KB:OFF_VB])
    for j in range(B_QK // LANES):
        kb_ref[0, :, j * LANES:(j + 1) * LANES] = rot(
            p[:, j * LANES:(j + 1) * LANES], HEAD_DIM).astype(BF16)
    p = _dot(h, w_ref[:, OFF_VB:OFF_QR]).astype(BF16)
    for j in range(B_HEADS):
        vb_ref[0, :, 2 * j * B_V_DIM:(2 * j + 1) * B_V_DIM] = p[:, j * B_V_DIM:(j + 1) * B_V_DIM]
        vb_ref[0, :, (2 * j + 1) * B_V_DIM:(2 * j + 2) * B_V_DIM] = jnp.ones(
            (p.shape[0], B_V_DIM), BF16)

    p = _dot(h, w_ref[:, OFF_QR:OFF_VR])
    qr_ref[0] = rot(p[:, :C_QK], C_QK_DIM).astype(BF16)
    kr_ref[0] = rot(p[:, C_QK:] * c_scale, C_QK_DIM).astype(BF16)
    vr_ref[0] = _dot(h, w_ref[:, OFF_VR:OFF_GR]).astype(BF16)
    gr_ref[0] = _dot(h, w_ref[:, OFF_GR:IN_WIDTH])


def _in_proj(x, prm, l, mod_row, tabs, tm):
    bsz, n, d = x.shape
    rope = tabs is not None
    in_specs = [
        pl.BlockSpec((1, tm, d), lambda b, i: (b, i, 0)),
        _layer_spec(prm["norm1_g"], l),
        mod_row(0),
        mod_row(1),
        _layer_spec(prm["w_in"], l),
    ]
    args = [x, prm["norm1_g"], prm["mod"], prm["mod"], prm["w_in"]]
    if rope:
        in_specs += [pl.BlockSpec((tm, LANES), lambda b, i: (i, 0))] * 4
        args += list(tabs)
    widths = (A_Q, A_KV, 2 * A_KV, B_QK, B_QK, 2 * B_V, C_QK, C_QK, C_V, C_V)
    dtypes = (BF16,) * 9 + (F32,)
    out_specs = [pl.BlockSpec((1, tm, wd), lambda b, i: (b, i, 0)) for wd in widths]
    out_shape = [jax.ShapeDtypeStruct((bsz, n, wd), dt) for wd, dt in zip(widths, dtypes)]
    return pl.pallas_call(
        functools.partial(_inproj_kernel, rope=rope),
        grid=(bsz, n // tm),
        in_specs=in_specs,
        out_specs=out_specs,
        out_shape=out_shape,
        compiler_params=_params("parallel", "parallel"),
        name="in_proj_rope" if rope else "in_proj_ctx",
    )(*args)


def _mixer_a_kernel(*refs, tq, n_lat, local, layer):
    if local:
        sink_ref, q_ref, k_ref, v_ref, kc_ref, vc_ref, o_ref = refs
    else:
        sink_ref, q_ref, kc_ref, vc_ref, o_ref = refs
    blk = A_BLOCK
    half = HEAD_DIM
    lane = lax.broadcasted_iota(jnp.int32, (blk, LANES), 1)
    low = lane < half
    rows = A_HEADS * blk
    row_head = lax.broadcasted_iota(jnp.int32, (rows, 1), 0) // blk
    sink = jnp.zeros((rows, 1), F32)
    for h in range(A_HEADS):
        sink = jnp.where(row_head == h, sink_ref[layer, h], sink)
    sink = sink * math.log2(math.e)
    kc = kc_ref[0]
    vc = vc_ref[0]
    if local:
        nb = n_lat // blk
        qrow = lax.broadcasted_iota(jnp.int32, (rows, blk), 0) & (blk - 1)
        kcol = lax.broadcasted_iota(jnp.int32, (rows, blk), 1)
        tri_prev = kcol >= qrow
        tri_next = kcol <= qrow
    for sub in range(tq // blk):
        r0 = sub * blk
        gb = pl.program_id(1) * (tq // blk) + sub
        if local:
            prev0 = pl.multiple_of(jnp.maximum(gb - 1, 0) * blk, blk)
            cur0 = pl.multiple_of(gb * blk, blk)
            next0 = pl.multiple_of(jnp.minimum(gb + 1, nb - 1) * blk, blk)
            k_all = jnp.concatenate([k_ref[0, pl.ds(prev0, blk), :], k_ref[0, pl.ds(cur0, blk), :],
                                     k_ref[0, pl.ds(next0, blk), :], kc], axis=0)
            v_all = jnp.concatenate([v_ref[0, pl.ds(prev0, blk), :], v_ref[0, pl.ds(cur0, blk), :],
                                     v_ref[0, pl.ds(next0, blk), :], vc], axis=0)
            ok_prev = jnp.logical_and(tri_prev, gb > 0)
            ok_next = jnp.logical_and(tri_next, gb < nb - 1)
        else:
            k_all, v_all = kc, vc
        qs = []
        for hk in range(A_KV_HEADS):
            qp = q_ref[0, r0:r0 + blk, hk * LANES:(hk + 1) * LANES].astype(F32)
            qsw = pltpu.roll(qp, half, 1)
            if hk == 0:
                qs += [jnp.where(low, qp, 0.0), jnp.where(low, qsw, 0.0)]
            else:
                qs += [jnp.where(low, 0.0, qsw), jnp.where(low, 0.0, qp)]
        qz = jnp.concatenate(qs, axis=0).astype(BF16)
        s = _nt_dot(qz, k_all)
        if local:
            s = jnp.concatenate([jnp.where(ok_prev, s[:, :blk], NEG_INF), s[:, blk:2 * blk],
                                 jnp.where(ok_next, s[:, 2 * blk:3 * blk], NEG_INF),
                                 s[:, 3 * blk:]], axis=1)
        m = jnp.maximum(jnp.max(s, axis=-1, keepdims=True), sink)
        r = _dot(jnp.exp2(s - m).astype(BF16), v_all)
        r = r[:, :LANES] / (r[:, LANES:] + jnp.exp2(sink - m))
        o_ref[0, r0:r0 + blk, :LANES] = jnp.where(
            low, r[:blk], pltpu.roll(r[blk:2 * blk], half, 1)).astype(BF16)
        o_ref[0, r0:r0 + blk, LANES:] = jnp.where(
            low, pltpu.roll(r[2 * blk:3 * blk], half, 1), r[3 * blk:]).astype(BF16)


def _mixer_a(prm, l, q, k, v, kc, vc, tq):
    bsz, n, _ = q.shape
    ctx_len = kc.shape[1]
    local = k is not None
    in_specs = [pl.BlockSpec(memory_space=pltpu.SMEM),
                pl.BlockSpec((1, tq, A_Q), lambda b, i: (b, i, 0))]
    args = [prm["sink"], q]
    if local:
        in_specs += [pl.BlockSpec((1, n, A_KV), lambda b, i: (b, 0, 0)),
                     pl.BlockSpec((1, n, 2 * A_KV), lambda b, i: (b, 0, 0))]
        args += [k, v]
    in_specs += [pl.BlockSpec((1, ctx_len, A_KV), lambda b, i: (b, 0, 0)),
                 pl.BlockSpec((1, ctx_len, 2 * A_KV), lambda b, i: (b, 0, 0))]
    args += [kc, vc]
    return pl.pallas_call(
        functools.partial(_mixer_a_kernel, tq=tq, n_lat=n, local=local, layer=l),
        grid=(bsz, n // tq),
        in_specs=in_specs,
        out_specs=pl.BlockSpec((1, tq, A_Q), lambda b, i: (b, i, 0)),
        out_shape=jax.ShapeDtypeStruct((bsz, n, A_Q), BF16),
        compiler_params=_params("parallel", "parallel"),
        name="mixer_a_window" if local else "mixer_a_ctx",
    )(*args)


def _mixer_b_kernel(*refs, tq, tk, n_lat, lam_init):
    if n_lat:
        lam_ref, g_ref, q_ref, kc_ref, vc_ref, k_ref, v_ref, o_ref, acc_ref = refs
    else:
        lam_ref, g_ref, q_ref, kc_ref, vc_ref, o_ref, acc_ref = refs
    q = q_ref[0]
    lane = lax.broadcasted_iota(jnp.int32, q.shape, 1)
    zero = jnp.zeros_like(q)
    qz = (jnp.where(lane < B_QK_DIM, q, zero), jnp.where(lane < B_QK_DIM, zero, q))

    def step(kblk, vblk, ms, first):
        out = []
        for c in range(2):
            s = _nt_dot(qz[c], kblk)
            m_new = jnp.maximum(ms[c], jnp.max(s, axis=-1, keepdims=True))
            p = jnp.exp2(s - m_new)
            pv = _dot(p.astype(BF16), vblk)
            if first:
                acc_ref[c] = pv
            else:
                acc_ref[c] = jnp.exp2(ms[c] - m_new) * acc_ref[c] + pv
            out.append(m_new)
        return tuple(out)

    m0 = jnp.full((tq, 1), NEG_INF, F32)
    ms = step(kc_ref[0], vc_ref[0], (m0, m0), True)
    for j in range(n_lat // tk):
        ms = step(k_ref[0, j * tk:(j + 1) * tk, :], v_ref[0, j * tk:(j + 1) * tk, :], ms, False)

    d = lam_ref[...]
    lam = (jnp.exp(jnp.sum(d[0:1] * d[1:2], axis=-1, keepdims=True))
           - jnp.exp(jnp.sum(d[2:3] * d[3:4], axis=-1, keepdims=True)) + lam_init)
    o = (acc_ref[0, :, :B_V_DIM] / acc_ref[0, :, B_V_DIM:]
         - lam * (acc_ref[1, :, :B_V_DIM] / acc_ref[1, :, B_V_DIM:]))
    y = _rms(o) * g_ref[...] * (1.0 - lam_init)
    o_ref[0] = y.astype(BF16)


def _mixer_b(prm, l, q, kc, vc, k, v, lam_init, tq, tk):
    bsz, n, _ = q.shape
    ctx_len = kc.shape[1]
    has_lat = k is not None
    n_lat = k.shape[1] if has_lat else 0
    in_specs = [
        _layer_spec(prm["lam"], l),
        _layer_spec(prm["subln_g"], l),
        pl.BlockSpec((1, tq, LANES), lambda b, h, i: (b, i, h)),
        pl.BlockSpec((1, ctx_len, LANES), lambda b, h, i: (b, 0, h)),
        pl.BlockSpec((1, ctx_len, 2 * B_V_DIM), lambda b, h, i: (b, 0, h)),
    ]
    args = [prm["lam"], prm["subln_g"], q, kc, vc]
    if has_lat:
        in_specs += [pl.BlockSpec((1, n_lat, LANES), lambda b, h, i: (b, 0, h)),
                     pl.BlockSpec((1, n_lat, 2 * B_V_DIM), lambda b, h, i: (b, 0, h))]
        args += [k, v]
    return pl.pallas_call(
        functools.partial(_mixer_b_kernel, tq=tq, tk=tk, n_lat=n_lat, lam_init=lam_init),
        grid=(bsz, B_HEADS, n // tq),
        in_specs=in_specs,
        out_specs=pl.BlockSpec((1, tq, LANES), lambda b, h, i: (b, i, h)),
        out_shape=jax.ShapeDtypeStruct((bsz, n, B_V), BF16),
        scratch_shapes=[pltpu.VMEM((2, tq, 2 * B_V_DIM), F32)],
        compiler_params=_params("parallel", "parallel", "parallel"),
        name="mixer_b_latent" if has_lat else "mixer_b_ctx",
    )(*args)


def _log_sigmoid(x):
    return jnp.minimum(x, 0.0) - jnp.log(1.0 + jnp.exp(-jnp.abs(x)))


def _per_head(ref, row, shape, axis, width):
    idx = lax.broadcasted_iota(jnp.int32, shape, axis) // width
    out = jnp.zeros(shape, F32)
    for h in range(C_HEADS):
        out = jnp.where(idx == h, ref[row + (h,)], out)
    return out


def _mixer_c_kernel(*refs, n, ctx_len, layer):
    if ctx_len:
        dec_ref, q_ref, k_ref, v_ref, g_ref, kc_ref, vc_ref, o_ref, sf_ref = refs
    else:
        dec_ref, q_ref, k_ref, v_ref, g_ref, o_ref, sf_ref = refs
    ch = C_CHUNK
    nc = n // ch
    qk_w, v_w = C_QK, C_V

    fwd_row, bwd_row = (layer, 0), (layer, 1)
    lgf_qk = _log_sigmoid(_per_head(dec_ref, fwd_row, (1, qk_w), 1, C_QK_DIM))
    lgb_qk = _log_sigmoid(_per_head(dec_ref, bwd_row, (1, qk_w), 1, C_QK_DIM))
    lgf_blk = _log_sigmoid(_per_head(dec_ref, fwd_row, (1, C_HEADS * ch), 1, ch))
    lgb_blk = _log_sigmoid(_per_head(dec_ref, bwd_row, (1, C_HEADS * ch), 1, ch))
    lgf_row = _log_sigmoid(_per_head(dec_ref, fwd_row, (qk_w, 1), 0, C_QK_DIM))
    lgb_row = _log_sigmoid(_per_head(dec_ref, bwd_row, (qk_w, 1), 0, C_QK_DIM))

    pos = lax.broadcasted_iota(jnp.int32, (ch, 1), 0).astype(F32)
    q_dec_f = jnp.exp(lgf_qk * (pos + 1.0))
    q_dec_b = jnp.exp(lgb_qk * (ch - pos))
    k_dec_f = jnp.exp(lgf_qk * (ch - 1.0 - pos))
    k_dec_b = jnp.exp(lgb_qk * pos)
    chunk_f = jnp.exp(lgf_row * float(ch))
    chunk_b = jnp.exp(lgb_row * float(ch))
    col = lax.broadcasted_iota(jnp.int32, (ch, C_HEADS * ch), 1) & (ch - 1)
    diff = (lax.broadcasted_iota(jnp.int32, (ch, C_HEADS * ch), 0) - col).astype(F32)
    decay = jnp.where(diff >= 0.0, jnp.exp(lgf_blk * jnp.maximum(diff, 0.0)),
                      jnp.exp(lgb_blk * jnp.maximum(-diff, 0.0)))
    state_mask = (lax.broadcasted_iota(jnp.int32, (qk_w, v_w), 0) // C_QK_DIM
                  == lax.broadcasted_iota(jnp.int32, (qk_w, v_w), 1) // C_V_DIM)
    head_qk = lax.broadcasted_iota(jnp.int32, (ch, qk_w), 1) // C_QK_DIM
    head_v = lax.broadcasted_iota(jnp.int32, (ch, v_w), 1) // C_V_DIM

    def kv_update(kf32, vb16, weights):
        kk = jnp.concatenate([kf32 * w for w in weights], axis=1) if len(weights) > 1 \
            else kf32 * weights[0]
        u = _dot(kk.T.astype(BF16), vb16)
        return [jnp.where(state_mask, u[j * qk_w:(j + 1) * qk_w], 0.0) for j in range(len(weights))]

    if ctx_len:
        cpos = lax.broadcasted_iota(jnp.int32, (ctx_len, 1), 0).astype(F32)
        s_f, s_b = kv_update(kc_ref[0].astype(F32), vc_ref[0],
                             [jnp.exp(lgf_qk * (ctx_len - 1.0 - cpos)), jnp.exp(lgb_qk * cpos)])
    else:
        s_f = jnp.zeros((qk_w, v_w), F32)
        s_b = jnp.zeros((qk_w, v_w), F32)

    def fwd(c, state):
        off = pl.multiple_of(c * ch, ch)
        sf_ref[c] = state.astype(BF16)
        (u_f,) = kv_update(k_ref[0, pl.ds(off, ch), :].astype(F32), v_ref[0, pl.ds(off, ch), :],
                           [k_dec_f])
        return chunk_f * state + u_f

    lax.fori_loop(0, nc, fwd, s_f, unroll=min(nc, 8))

    def bwd(t, state_b):
        c = nc - 1 - t
        off = pl.multiple_of(c * ch, ch)
        q = q_ref[0, pl.ds(off, ch), :]
        k = k_ref[0, pl.ds(off, ch), :]
        v = v_ref[0, pl.ds(off, ch), :]
        zq = jnp.zeros_like(k)
        zv = jnp.zeros_like(v)
        kz = jnp.concatenate([jnp.where(head_qk == h, k, zq) for h in range(C_HEADS)], axis=0)
        vz = jnp.concatenate([jnp.where(head_v == h, v, zv) for h in range(C_HEADS)], axis=0)
        att = (_nt_dot(q, kz) * decay).astype(BF16)
        y = _dot(att, vz)
        qf = q.astype(F32)
        qq = jnp.concatenate([qf * q_dec_f, qf * q_dec_b], axis=1).astype(BF16)
        states = jnp.concatenate([sf_ref[c], state_b.astype(BF16)], axis=0)
        y = y + _dot(qq, states)
        outs = []
        for j in range(v_w // LANES):
            yb = y[:, j * LANES:(j + 1) * LANES]
            lo = lax.broadcasted_iota(jnp.int32, yb.shape, 1) < C_V_DIM
            s_all = jnp.sum(yb, axis=-1, keepdims=True)
            s_lo = jnp.sum(jnp.where(lo, yb, 0.0), axis=-1, keepdims=True)
            yc = yb - jnp.where(lo, s_lo, s_all - s_lo) * (1.0 / C_V_DIM)
            sq = yc * yc
            q_all = jnp.sum(sq, axis=-1, keepdims=True)
            q_lo = jnp.sum(jnp.where(lo, sq, 0.0), axis=-1, keepdims=True)
            var = jnp.where(lo, q_lo, q_all - q_lo) * (1.0 / C_V_DIM)
            outs.append(yc * lax.rsqrt(var + NORM_EPS))
        yn = jnp.concatenate(outs, axis=1)
        gate = g_ref[0, pl.ds(off, ch), :]
        o_ref[0, pl.ds(off, ch), :] = (gate * _sigmoid(gate) * yn).astype(BF16)
        (u_b,) = kv_update(k.astype(F32), v, [k_dec_b])
        return chunk_b * state_b + u_b

    lax.fori_loop(0, nc, bwd, s_b, unroll=min(nc, 4))


def _mixer_c(prm, l, q, k, v, g, kc, vc):
    bsz, n, _ = q.shape
    ctx_len = kc.shape[1] if kc is not None else 0
    seq = lambda wd: pl.BlockSpec((1, n, wd), lambda b: (b, 0, 0))
    once = lambda wd: pl.BlockSpec((1, n, wd), lambda b: (b, 0, 0), pipeline_mode=pl.Buffered(1))
    in_specs = [pl.BlockSpec(memory_space=pltpu.SMEM), once(C_QK), once(C_QK), once(C_V), once(C_V)]
    args = [prm["decay"], q, k, v, g]
    if ctx_len:
        in_specs += [pl.BlockSpec((1, ctx_len, C_QK), lambda b: (b, 0, 0)),
                     pl.BlockSpec((1, ctx_len, C_V), lambda b: (b, 0, 0))]
        args += [kc, vc]
    return pl.pallas_call(
        functools.partial(_mixer_c_kernel, n=n, ctx_len=ctx_len, layer=l),
        grid=(bsz,),
        in_specs=in_specs,
        out_specs=seq(C_V),
        out_shape=jax.ShapeDtypeStruct((bsz, n, C_V), BF16),
        scratch_shapes=[pltpu.VMEM((n // C_CHUNK, C_QK, C_V), BF16)],
        compiler_params=_params("parallel"),
        name="mixer_c_latent" if ctx_len else "mixer_c_ctx",
    )(*args)


FF_CHUNK = 256
HALO = 2 * SUBLANES


def _post_kernel(*refs, tm, d_ff, final):
    (am, ap, an, bm, bp, bn, cm, cp, cn, xm, xp, xn_, wo_ref, g1_ref, n2_ref, sh_ref, sc_ref,
     wup_ref, cw_ref, cb_ref, wdn_ref, g2_ref, fg_ref, o_ref, mix_ref, xs_ref, hs_ref, act_ref) = refs
    i = pl.program_id(1)
    last = pl.num_programs(1) - 1
    rows = tm + 2 * HALO
    col = 0
    for main, prev, nxt in ((am, ap, an), (bm, bp, bn), (cm, cp, cn)):
        wd = main.shape[-1]
        mix_ref[0:HALO, col:col + wd] = prev[0]
        mix_ref[HALO:HALO + tm, col:col + wd] = main[0]
        mix_ref[HALO + tm:rows, col:col + wd] = nxt[0]
        col += wd
    xs_ref[0:HALO, :] = xp[0]
    xs_ref[HALO:HALO + tm, :] = xm[0]
    xs_ref[HALO + tm:rows, :] = xn_[0]

    xs_ref[...] = xs_ref[...] + g1_ref[0] * _dot(mix_ref[...], wo_ref[...])
    h = (_rms(xs_ref[...]) * n2_ref[...]) * (1.0 + sc_ref[0]) + sh_ref[0]
    hs_ref[...] = h.astype(BF16)
    zero = jnp.zeros((HALO, h.shape[1]), BF16)

    @pl.when(i == 0)
    def _():
        hs_ref[0:HALO, :] = zero

    @pl.when(i == last)
    def _():
        hs_ref[HALO + tm:rows, :] = zero

    hs = hs_ref[...]

    def conv(u, off):
        w = cw_ref[:, off:off + FF_CHUNK]
        return (u[HALO - 1:HALO - 1 + tm] * w[0:1] + u[HALO:HALO + tm] * w[1:2]
                + u[HALO + 1:HALO + 1 + tm] * w[2:3] + cb_ref[:, off:off + FF_CHUNK])

    for c in range(d_ff // FF_CHUNK):
        lo = c * FF_CHUNK
        val = conv(_dot(hs, wup_ref[:, lo:lo + FF_CHUNK]), lo)
        gate = conv(_dot(hs, wup_ref[:, d_ff + lo:d_ff + lo + FF_CHUNK]), d_ff + lo)
        act_ref[:, lo:lo + FF_CHUNK] = (gate * _sigmoid(gate) * val).astype(BF16)
    y = _dot(act_ref[...], wdn_ref[...])
    out = xs_ref[HALO:HALO + tm, :] + g2_ref[0] * y
    if final:
        out = _rms(out) * fg_ref[...]
    o_ref[0] = out


def _post(a, b, c, x, prm, l, mod_row, tm, final):
    bsz, n, d = x.shape
    d_ff = prm["w_down"].shape[1]
    nb = tm // HALO
    n_halo_blocks = n // HALO
    layer = lambda name: _layer_spec(prm[name], l, single_buffer=True)

    def tiles(wd):
        return [pl.BlockSpec((1, tm, wd), lambda bb, i: (bb, i, 0)),
                pl.BlockSpec((1, HALO, wd), lambda bb, i: (bb, jnp.maximum(i * nb - 1, 0), 0)),
                pl.BlockSpec((1, HALO, wd),
                             lambda bb, i: (bb, jnp.minimum((i + 1) * nb, n_halo_blocks - 1), 0))]

    rows = tm + 2 * HALO
    return pl.pallas_call(
        functools.partial(_post_kernel, tm=tm, d_ff=d_ff, final=final),
        grid=(bsz, n // tm),
        in_specs=(tiles(A_Q) + tiles(B_V) + tiles(C_V) + tiles(d)
                  + [layer("w_out"), mod_row(2), layer("norm2_g"), mod_row(3), mod_row(4),
                     layer("w_up"), layer("conv_w"), layer("conv_b"), layer("w_down"), mod_row(5),
                     pl.BlockSpec((1, d), lambda bb, i: (0, 0))]),
        out_specs=pl.BlockSpec((1, tm, d), lambda bb, i: (bb, i, 0)),
        out_shape=jax.ShapeDtypeStruct((bsz, n, d), F32),
        scratch_shapes=[pltpu.VMEM((rows, d), BF16), pltpu.VMEM((rows, d), F32),
                        pltpu.VMEM((rows, d), BF16), pltpu.VMEM((tm, d_ff), BF16)],
        compiler_params=_params("parallel", "parallel", vmem=POST_VMEM_LIMIT),
        name="post_final" if final else "post",
    )(a, a, a, b, b, b, c, c, c, x, x, x, prm["w_out"], prm["mod"], prm["norm2_g"], prm["mod"],
      prm["mod"], prm["w_up"], prm["conv_w"], prm["conv_b"], prm["w_down"], prm["mod"],
      prm["final_g"])


def _rope_tables(n_tok, dim):
    n_rows = n_tok // GRID_W
    rows = jnp.repeat(jnp.arange(n_rows, dtype=F32), GRID_W)
    cols = jnp.tile(jnp.arange(GRID_W, dtype=F32), n_rows)
    n_freq = dim // 4
    inv = ROPE_BASE ** (-jnp.arange(n_freq, dtype=F32) / n_freq)
    ang = jnp.concatenate([rows[:, None] * inv, cols[:, None] * inv], axis=-1)
    cos, sin = jnp.cos(ang), jnp.sin(ang)
    reps = LANES // dim
    cos_t = jnp.tile(jnp.concatenate([cos, cos], axis=-1), (1, reps))
    sin_t = jnp.tile(jnp.concatenate([-sin, sin], axis=-1), (1, reps))
    return cos_t, sin_t


def _tile_rows(n, target):
    t = min(n, target)
    while n % t:
        t //= 2
    return t


def kernel(x, c, ctx, c_ctx, w_mod, b_mod, norm1_g, norm2_g, w_in, w_out, attn_sink,
           diff_lambda, diff_subln_g, ret_decay_logit, w_up, conv_w, conv_b, w_down, final_g):
    bsz, n_lat, d = x.shape
    depth = w_mod.shape[0]
    tabs = _rope_tables(n_lat, HEAD_DIM) + _rope_tables(n_lat, C_QK_DIM)

    rows = -(-(bsz + 1) // SUBLANES) * SUBLANES
    cc = jnp.concatenate([c, c_ctx[None, :], jnp.zeros((rows - bsz - 1, d), F32)], axis=0)
    mod = _modulation(cc, w_mod, b_mod)
    mod = mod.reshape(depth * rows * N_MOD, 1, d)

    prm = dict(
        w_in=w_in.astype(BF16), w_out=w_out.astype(BF16), w_up=w_up.astype(BF16),
        w_down=w_down.astype(BF16), conv_w=conv_w, conv_b=conv_b[:, None, :],
        norm1_g=norm1_g[:, None, :], norm2_g=norm2_g[:, None, :],
        subln_g=diff_subln_g[:, None, :], lam=diff_lambda, sink=attn_sink,
        decay=ret_decay_logit, final_g=final_g.reshape(1, d), mod=mod)

    tm_lat = _tile_rows(n_lat, 1024)
    tm_ctx = _tile_rows(ctx.shape[1], 1024)
    xc = ctx
    for l in range(depth):
        is_last = l == depth - 1
        lam_init = 0.8 - 0.6 * math.exp(-0.3 * l)

        def mod_row(j, is_ctx, l=l):
            if is_ctx:
                return pl.BlockSpec((1, 1, d), lambda b, *_: ((l * rows + bsz) * N_MOD + j, 0, 0))
            return pl.BlockSpec((1, 1, d), lambda b, *_: ((l * rows + b) * N_MOD + j, 0, 0))

        lat = functools.partial(mod_row, is_ctx=False)
        cmod = functools.partial(mod_row, is_ctx=True)

        (qa, ka, va, qb, kb, vb, qr, kr, vr, gr) = _in_proj(x, prm, l, lat, tabs, tm_lat)
        (qa_c, ka_c, va_c, qb_c, kb_c, vb_c, qr_c, kr_c, vr_c, gr_c) = _in_proj(
            xc, prm, l, cmod, None, tm_ctx)

        a_out = _mixer_a(prm, l, qa, ka, va, ka_c, va_c, tm_lat)
        b_out = _mixer_b(prm, l, qb, kb_c, vb_c, kb, vb, lam_init, tm_lat, _tile_rows(n_lat, 512))
        c_out = _mixer_c(prm, l, qr, kr, vr, gr, kr_c, vr_c)
        x = _post(a_out, b_out, c_out, x, prm, l, lat, tm_lat, is_last)

        if not is_last:
            a_c = _mixer_a(prm, l, qa_c, None, None, ka_c, va_c, tm_ctx)
            b_c = _mixer_b(prm, l, qb_c, kb_c, vb_c, None, None, lam_init, tm_ctx, tm_ctx)
            c_c = _mixer_c(prm, l, qr_c, kr_c, vr_c, gr_c, None, None)
            xc = _post(a_c, b_c, c_c, xc, prm, l, cmod, tm_ctx, False)
    return x
```

```python
import functools
import math

import jax
import jax.numpy as jnp
from jax import lax
from jax.experimental import pallas as pl
from jax.experimental.pallas import tpu as pltpu

F32 = jnp.float32
BF16 = jnp.bfloat16

GRID_W = 64
HEAD_DIM = 64
ROPE_BASE = 10000.0
NORM_EPS = 1e-6
NEG_INF = -1e30

A_HEADS = 4
A_KV_HEADS = 2
A_WINDOW = 128
A_BLOCK = 128
B_HEADS = 4
B_QK_DIM = 64
B_V_DIM = 128
C_HEADS = 4
C_QK_DIM = 32
C_V_DIM = 64
C_CHUNK = 128

A_Q = A_HEADS * HEAD_DIM
A_KV = A_KV_HEADS * HEAD_DIM
B_QK = B_HEADS * 2 * B_QK_DIM
B_V = B_HEADS * B_V_DIM
C_QK = C_HEADS * C_QK_DIM
C_V = C_HEADS * C_V_DIM
IN_WIDTH = A_Q + 2 * A_KV + 2 * B_QK + B_V + 2 * C_QK + 2 * C_V
N_MOD = 6
CONV_W = 3

assert A_WINDOW == A_BLOCK

LANES = 128
SUBLANES = 8
VMEM_LIMIT = 56 * 1024 * 1024
POST_VMEM_LIMIT = 62 * 1024 * 1024

OFF_QA = 0
OFF_QB = A_Q + 2 * A_KV
OFF_KB = OFF_QB + B_QK
OFF_VB = OFF_KB + B_QK
OFF_QR = OFF_VB + B_V
OFF_VR = OFF_QR + 2 * C_QK
OFF_GR = OFF_VR + C_V


def _params(*sem, vmem=VMEM_LIMIT):
    return pltpu.CompilerParams(dimension_semantics=sem, vmem_limit_bytes=vmem)


def _layer_spec(arr, l, single_buffer=False):
    zeros = (0,) * (arr.ndim - 1)
    mode = dict(pipeline_mode=pl.Buffered(1)) if single_buffer else {}
    return pl.BlockSpec((None,) + arr.shape[1:], lambda *_: (l,) + zeros, **mode)


def _nt_dot(a, b):
    return lax.dot_general(a, b, (((1,), (1,)), ((), ())), preferred_element_type=F32)


def _dot(a, b):
    return jnp.dot(a, b, preferred_element_type=F32)


def _rms(x):
    return x * lax.rsqrt(jnp.mean(x * x, axis=-1, keepdims=True) + NORM_EPS)


def _sigmoid(x):
    return jax.nn.sigmoid(x)


def _mod_kernel(c_ref, w_ref, b_ref, o_ref):
    c = c_ref[...]
    a = (c * _sigmoid(c)).astype(BF16)
    o_ref[0] = _dot(a, w_ref[0].astype(BF16)) + b_ref[0]


def _modulation(cc, w_mod, b_mod):
    depth, d, width = w_mod.shape
    rows = cc.shape[0]
    blk = d
    return pl.pallas_call(
        _mod_kernel,
        grid=(depth, width // blk),
        in_specs=[
            pl.BlockSpec((rows, d), lambda l, j: (0, 0)),
            pl.BlockSpec((1, d, blk), lambda l, j: (l, 0, j)),
            pl.BlockSpec((1, 1, blk), lambda l, j: (l, 0, j)),
        ],
        out_specs=pl.BlockSpec((1, rows, blk), lambda l, j: (l, 0, j)),
        out_shape=jax.ShapeDtypeStruct((depth, rows, width), F32),
        compiler_params=_params("arbitrary", "arbitrary"),
        name="modulation",
    )(cc, w_mod, b_mod.reshape(depth, 1, width))


def _rope_block(blk, cos, sin_signed, half):
    lane = lax.broadcasted_iota(jnp.int32, blk.shape, 1)
    first = (lane & (2 * half - 1)) < half
    swapped = jnp.where(first, pltpu.roll(blk, LANES - half, 1), pltpu.roll(blk, half, 1))
    return blk * cos + swapped * sin_signed


def _inproj_kernel(*refs, rope):
    if rope:
        (x_ref, g_ref, sh_ref, sc_ref, w_ref, c64_ref, s64_ref, c32_ref, s32_ref,
         qa_ref, ka_ref, va_ref, qb_ref, kb_ref, vb_ref, qr_ref, kr_ref, vr_ref, gr_ref) = refs
    else:
        (x_ref, g_ref, sh_ref, sc_ref, w_ref,
         qa_ref, ka_ref, va_ref, qb_ref, kb_ref, vb_ref, qr_ref, kr_ref, vr_ref, gr_ref) = refs
    x = x_ref[0]
    h = (_rms(x) * g_ref[...]) * (1.0 + sc_ref[0]) + sh_ref[0]
    h = h.astype(BF16)

    def rot(blk, dim):
        if not rope:
            return blk
        if dim == HEAD_DIM:
            return _rope_block(blk, c64_ref[...], s64_ref[...], HEAD_DIM // 2)
        return _rope_block(blk, c32_ref[...], s32_ref[...], C_QK_DIM // 2)

    a_scale = HEAD_DIM ** -0.5 * math.log2(math.e)
    b_scale = B_QK_DIM ** -0.5 * math.log2(math.e)
    c_scale = C_QK_DIM ** -0.5

    p = _dot(h, w_ref[:, OFF_QA:OFF_QB])
    for j in range(A_Q // LANES):
        qa_ref[0, :, j * LANES:(j + 1) * LANES] = rot(
            p[:, j * LANES:(j + 1) * LANES] * a_scale, HEAD_DIM).astype(BF16)
    ka_ref[0] = rot(p[:, A_Q:A_Q + A_KV], HEAD_DIM).astype(BF16)
    va_ref[0, :, :A_KV] = p[:, A_Q + A_KV:A_Q + 2 * A_KV].astype(BF16)
    va_ref[0, :, A_KV:] = jnp.ones((p.shape[0], A_KV), BF16)

    p = _dot(h, w_ref[:, OFF_QB:OFF_KB])
    for j in range(B_QK // LANES):
        qb_ref[0, :, j * LANES:(j + 1) * LANES] = rot(
            p[:, j * LANES:(j + 1) * LANES] * b_scale, HEAD_DIM).astype(BF16)
    p = _dot(h, w_ref[:, OFF_KB:OFF_VB])
    for j in range(B_QK // LANES):
        kb_ref[0, :, j * LANES:(j + 1) * LANES] = rot(
            p[:, j * LANES:(j + 1) * LANES], HEAD_DIM).astype(BF16)
    p = _dot(h, w_ref[:, OFF_VB:OFF_QR]).astype(BF16)
    for j in range(B_HEADS):
        vb_ref[0, :, 2 * j * B_V_DIM:(2 * j + 1) * B_V_DIM] = p[:, j * B_V_DIM:(j + 1) * B_V_DIM]
        vb_ref[0, :, (2 * j + 1) * B_V_DIM:(2 * j + 2) * B_V_DIM] = jnp.ones(
            (p.shape[0], B_V_DIM), BF16)

    p = _dot(h, w_ref[:, OFF_QR:OFF_VR])
    qr_ref[0] = rot(p[:, :C_QK], C_QK_DIM).astype(BF16)
    kr_ref[0] = rot(p[:, C_QK:] * c_scale, C_QK_DIM).astype(BF16)
    vr_ref[0] = _dot(h, w_ref[:, OFF_VR:OFF_GR]).astype(BF16)
    gr_ref[0] = _dot(h, w_ref[:, OFF_GR:IN_WIDTH])


def _in_proj(x, prm, l, mod_row, tabs, tm):
    bsz, n, d = x.shape
    rope = tabs is not None
    in_specs = [
        pl.BlockSpec((1, tm, d), lambda b, i: (b, i, 0)),
        _layer_spec(prm["norm1_g"], l),
        mod_row(0),
        mod_row(1),
        _layer_spec(prm["w_in"], l),
    ]
    args = [x, prm["norm1_g"], prm["mod"], prm["mod"], prm["w_in"]]
    if rope:
        in_specs += [pl.BlockSpec((tm, LANES), lambda b, i: (i, 0))] * 4
        args += list(tabs)
    widths = (A_Q, A_KV, 2 * A_KV, B_QK, B_QK, 2 * B_V, C_QK, C_QK, C_V, C_V)
    dtypes = (BF16,) * 9 + (F32,)
    out_specs = [pl.BlockSpec((1, tm, wd), lambda b, i: (b, i, 0)) for wd in widths]
    out_shape = [jax.ShapeDtypeStruct((bsz, n, wd), dt) for wd, dt in zip(widths, dtypes)]
    return pl.pallas_call(
        functools.partial(_inproj_kernel, rope=rope),
        grid=(bsz, n // tm),
        in_specs=in_specs,
        out_specs=out_specs,
        out_shape=out_shape,
        compiler_params=_params("parallel", "parallel"),
        name="in_proj_rope" if rope else "in_proj_ctx",
    )(*args)


def _mixer_a_kernel(*refs, tq, n_lat, local, layer):
    if local:
        sink_ref, q_ref, k_ref, v_ref, kc_ref, vc_ref, o_ref = refs
    else:
        sink_ref, q_ref, kc_ref, vc_ref, o_ref = refs
    blk = A_BLOCK
    half = HEAD_DIM
    lane = lax.broadcasted_iota(jnp.int32, (blk, LANES), 1)
    low = lane < half
    rows = A_HEADS * blk
    row_head = lax.broadcasted_iota(jnp.int32, (rows, 1), 0) // blk
    sink = jnp.zeros((rows, 1), F32)
    for h in range(A_HEADS):
        sink = jnp.where(row_head == h, sink_ref[layer, h], sink)
    sink = sink * math.log2(math.e)
    kc = kc_ref[0]
    vc = vc_ref[0]
    if local:
        nb = n_lat // blk
        qrow = lax.broadcasted_iota(jnp.int32, (rows, blk), 0) & (blk - 1)
        kcol = lax.broadcasted_iota(jnp.int32, (rows, blk), 1)
        tri_prev = kcol >= qrow
        tri_next = kcol <= qrow
    for sub in range(tq // blk):
        r0 = sub * blk
        gb = pl.program_id(1) * (tq // blk) + sub
        if local:
            prev0 = pl.multiple_of(jnp.maximum(gb - 1, 0) * blk, blk)
            cur0 = pl.multiple_of(gb * blk, blk)
            next0 = pl.multiple_of(jnp.minimum(gb + 1, nb - 1) * blk, blk)
            k_all = jnp.concatenate([k_ref[0, pl.ds(prev0, blk), :], k_ref[0, pl.ds(cur0, blk), :],
                                     k_ref[0, pl.ds(next0, blk), :], kc], axis=0)
            v_all = jnp.concatenate([v_ref[0, pl.ds(prev0, blk), :], v_ref[0, pl.ds(cur0, blk), :],
                                     v_ref[0, pl.ds(next0, blk), :], vc], axis=0)
            ok_prev = jnp.logical_and(tri_prev, gb > 0)
            ok_next = jnp.logical_and(tri_next, gb < nb - 1)
        else:
            k_all, v_all = kc, vc
        qs = []
        for hk in range(A_KV_HEADS):
            qp = q_ref[0, r0:r0 + blk, hk * LANES:(hk + 1) * LANES].astype(F32)
            qsw = pltpu.roll(qp, half, 1)
            if hk == 0:
                qs += [jnp.where(low, qp, 0.0), jnp.where(low, qsw, 0.0)]
            else:
                qs += [jnp.where(low, 0.0, qsw), jnp.where(low, 0.0, qp)]
        qz = jnp.concatenate(qs, axis=0).astype(BF16)
        s = _nt_dot(qz, k_all)
        if local:
            s = jnp.concatenate([jnp.where(ok_prev, s[:, :blk], NEG_INF), s[:, blk:2 * blk],
                                 jnp.where(ok_next, s[:, 2 * blk:3 * blk], NEG_INF),
                                 s[:, 3 * blk:]], axis=1)
        m = jnp.maximum(jnp.max(s, axis=-1, keepdims=True), sink)
        r = _dot(jnp.exp2(s - m).astype(BF16), v_all)
        r = r[:, :LANES] / (r[:, LANES:] + jnp.exp2(sink - m))
        o_ref[0, r0:r0 + blk, :LANES] = jnp.where(
            low, r[:blk], pltpu.roll(r[blk:2 * blk], half, 1)).astype(BF16)
        o_ref[0, r0:r0 + blk, LANES:] = jnp.where(
            low, pltpu.roll(r[2 * blk:3 * blk], half, 1), r[3 * blk:]).astype(BF16)


def _mixer_a(prm, l, q, k, v, kc, vc, tq):
    bsz, n, _ = q.shape
    ctx_len = kc.shape[1]
    local = k is not None
    in_specs = [pl.BlockSpec(memory_space=pltpu.SMEM),
                pl.BlockSpec((1, tq, A_Q), lambda b, i: (b, i, 0))]
    args = [prm["sink"], q]
    if local:
        in_specs += [pl.BlockSpec((1, n, A_KV), lambda b, i: (b, 0, 0)),
                     pl.BlockSpec((1, n, 2 * A_KV), lambda b, i: (b, 0, 0))]
        args += [k, v]
    in_specs += [pl.BlockSpec((1, ctx_len, A_KV), lambda b, i: (b, 0, 0)),
                 pl.BlockSpec((1, ctx_len, 2 * A_KV), lambda b, i: (b, 0, 0))]
    args += [kc, vc]
    return pl.pallas_call(
        functools.partial(_mixer_a_kernel, tq=tq, n_lat=n, local=local, layer=l),
        grid=(bsz, n // tq),
        in_specs=in_specs,
        out_specs=pl.BlockSpec((1, tq, A_Q), lambda b, i: (b, i, 0)),
        out_shape=jax.ShapeDtypeStruct((bsz, n, A_Q), BF16),
        compiler_params=_params("parallel", "parallel"),
        name="mixer_a_window" if local else "mixer_a_ctx",
    )(*args)


def _mixer_b_kernel(*refs, tq, tk, n_lat, lam_init):
    if n_lat:
        lam_ref, g_ref, q_ref, kc_ref, vc_ref, k_ref, v_ref, o_ref, acc_ref = refs
    else:
        lam_ref, g_ref, q_ref, kc_ref, vc_ref, o_ref, acc_ref = refs
    q = q_ref[0]
    lane = lax.broadcasted_iota(jnp.int32, q.shape, 1)
    zero = jnp.zeros_like(q)
    qz = (jnp.where(lane < B_QK_DIM, q, zero), jnp.where(lane < B_QK_DIM, zero, q))

    def step(kblk, vblk, ms, first):
        out = []
        for c in range(2):
            s = _nt_dot(qz[c], kblk)
            m_new = jnp.maximum(ms[c], jnp.max(s, axis=-1, keepdims=True))
            p = jnp.exp2(s - m_new)
            pv = _dot(p.astype(BF16), vblk)
            if first:
                acc_ref[c] = pv
            else:
                acc_ref[c] = jnp.exp2(ms[c] - m_new) * acc_ref[c] + pv
            out.append(m_new)
        return tuple(out)

    m0 = jnp.full((tq, 1), NEG_INF, F32)
    ms = step(kc_ref[0], vc_ref[0], (m0, m0), True)
    for j in range(n_lat // tk):
        ms = step(k_ref[0, j * tk:(j + 1) * tk, :], v_ref[0, j * tk:(j + 1) * tk, :], ms, False)

    d = lam_ref[...]
    lam = (jnp.exp(jnp.sum(d[0:1] * d[1:2], axis=-1, keepdims=True))
           - jnp.exp(jnp.sum(d[2:3] * d[3:4], axis=-1, keepdims=True)) + lam_init)
    o = (acc_ref[0, :, :B_V_DIM] / acc_ref[0, :, B_V_DIM:]
         - lam * (acc_ref[1, :, :B_V_DIM] / acc_ref[1, :, B_V_DIM:]))
    y = _rms(o) * g_ref[...] * (1.0 - lam_init)
    o_ref[0] = y.astype(BF16)


def _mixer_b(prm, l, q, kc, vc, k, v, lam_init, tq, tk):
    bsz, n, _ = q.shape
    ctx_len = kc.shape[1]
    has_lat = k is not None
    n_lat = k.shape[1] if has_lat else 0
    in_specs = [
        _layer_spec(prm["lam"], l),
        _layer_spec(prm["subln_g"], l),
        pl.BlockSpec((1, tq, LANES), lambda b, h, i: (b, i, h)),
        pl.BlockSpec((1, ctx_len, LANES), lambda b, h, i: (b, 0, h)),
        pl.BlockSpec((1, ctx_len, 2 * B_V_DIM), lambda b, h, i: (b, 0, h)),
    ]
    args = [prm["lam"], prm["subln_g"], q, kc, vc]
    if has_lat:
        in_specs += [pl.BlockSpec((1, n_lat, LANES), lambda b, h, i: (b, 0, h)),
                     pl.BlockSpec((1, n_lat, 2 * B_V_DIM), lambda b, h, i: (b, 0, h))]
        args += [k, v]
    return pl.pallas_call(
        functools.partial(_mixer_b_kernel, tq=tq, tk=tk, n_lat=n_lat, lam_init=lam_init),
        grid=(bsz, B_HEADS, n // tq),
        in_specs=in_specs,
        out_specs=pl.BlockSpec((1, tq, LANES), lambda b, h, i: (b, i, h)),
        out_shape=jax.ShapeDtypeStruct((bsz, n, B_V), BF16),
        scratch_shapes=[pltpu.VMEM((2, tq, 2 * B_V_DIM), F32)],
        compiler_params=_params("parallel", "parallel", "parallel"),
        name="mixer_b_latent" if has_lat else "mixer_b_ctx",
    )(*args)


def _log_sigmoid(x):
    return jnp.minimum(x, 0.0) - jnp.log(1.0 + jnp.exp(-jnp.abs(x)))


def _per_head(ref, row, shape, axis, width):
    idx = lax.broadcasted_iota(jnp.int32, shape, axis) // width
    out = jnp.zeros(shape, F32)
    for h in range(C_HEADS):
        out = jnp.where(idx == h, ref[row + (h,)], out)
    return out


def _mixer_c_kernel(*refs, n, ctx_len, layer):
    if ctx_len:
        dec_ref, q_ref, k_ref, v_ref, g_ref, kc_ref, vc_ref, o_ref, sf_ref = refs
    else:
        dec_ref, q_ref, k_ref, v_ref, g_ref, o_ref, sf_ref = refs
    ch = C_CHUNK
    nc = n // ch
    qk_w, v_w = C_QK, C_V

    fwd_row, bwd_row = (layer, 0), (layer, 1)
    lgf_qk = _log_sigmoid(_per_head(dec_ref, fwd_row, (1, qk_w), 1, C_QK_DIM))
    lgb_qk = _log_sigmoid(_per_head(dec_ref, bwd_row, (1, qk_w), 1, C_QK_DIM))
    lgf_blk = _log_sigmoid(_per_head(dec_ref, fwd_row, (1, C_HEADS * ch), 1, ch))
    lgb_blk = _log_sigmoid(_per_head(dec_ref, bwd_row, (1, C_HEADS * ch), 1, ch))
    lgf_row = _log_sigmoid(_per_head(dec_ref, fwd_row, (qk_w, 1), 0, C_QK_DIM))
    lgb_row = _log_sigmoid(_per_head(dec_ref, bwd_row, (qk_w, 1), 0, C_QK_DIM))

    pos = lax.broadcasted_iota(jnp.int32, (ch, 1), 0).astype(F32)
    q_dec_f = jnp.exp(lgf_qk * (pos + 1.0))
    q_dec_b = jnp.exp(lgb_qk * (ch - pos))
    k_dec_f = jnp.exp(lgf_qk * (ch - 1.0 - pos))
    k_dec_b = jnp.exp(lgb_qk * pos)
    chunk_f = jnp.exp(lgf_row * float(ch))
    chunk_b = jnp.exp(lgb_row * float(ch))
    col = lax.broadcasted_iota(jnp.int32, (ch, C_HEADS * ch), 1) & (ch - 1)
    diff = (lax.broadcasted_iota(jnp.int32, (ch, C_HEADS * ch), 0) - col).astype(F32)
    decay = jnp.where(diff >= 0.0, jnp.exp(lgf_blk * jnp.maximum(diff, 0.0)),
                      jnp.exp(lgb_blk * jnp.maximum(-diff, 0.0)))
    state_mask = (lax.broadcasted_iota(jnp.int32, (qk_w, v_w), 0) // C_QK_DIM
                  == lax.broadcasted_iota(jnp.int32, (qk_w, v_w), 1) // C_V_DIM)
    head_qk = lax.broadcasted_iota(jnp.int32, (ch, qk_w), 1) // C_QK_DIM
    head_v = lax.broadcasted_iota(jnp.int32, (ch, v_w), 1) // C_V_DIM

    def kv_update(kf32, vb16, weights):
        kk = jnp.concatenate([kf32 * w for w in weights], axis=1) if len(weights) > 1 \
            else kf32 * weights[0]
        u = _dot(kk.T.astype(BF16), vb16)
        return [jnp.where(state_mask, u[j * qk_w:(j + 1) * qk_w], 0.0) for j in range(len(weights))]

    if ctx_len:
        cpos = lax.broadcasted_iota(jnp.int32, (ctx_len, 1), 0).astype(F32)
        s_f, s_b = kv_update(kc_ref[0].astype(F32), vc_ref[0],
                             [jnp.exp(lgf_qk * (ctx_len - 1.0 - cpos)), jnp.exp(lgb_qk * cpos)])
    else:
        s_f = jnp.zeros((qk_w, v_w), F32)
        s_b = jnp.zeros((qk_w, v_w), F32)

    def fwd(c, state):
        off = pl.multiple_of(c * ch, ch)
        sf_ref[c] = state.astype(BF16)
        (u_f,) = kv_update(k_ref[0, pl.ds(off, ch), :].astype(F32), v_ref[0, pl.ds(off, ch), :],
                           [k_dec_f])
        return chunk_f * state + u_f

    lax.fori_loop(0, nc, fwd, s_f, unroll=min(nc, 16))

    def bwd(t, state_b):
        c = nc - 1 - t
        off = pl.multiple_of(c * ch, ch)
        q = q_ref[0, pl.ds(off, ch), :]
        k = k_ref[0, pl.ds(off, ch), :]
        v = v_ref[0, pl.ds(off, ch), :]
        zq = jnp.zeros_like(k)
        zv = jnp.zeros_like(v)
        kz = jnp.concatenate([jnp.where(head_qk == h, k, zq) for h in range(C_HEADS)], axis=0)
        vz = jnp.concatenate([jnp.where(head_v == h, v, zv) for h in range(C_HEADS)], axis=0)
        att = (_nt_dot(q, kz) * decay).astype(BF16)
        y = _dot(att, vz)
        qf = q.astype(F32)
        qq = jnp.concatenate([qf * q_dec_f, qf * q_dec_b], axis=1).astype(BF16)
        states = jnp.concatenate([sf_ref[c], state_b.astype(BF16)], axis=0)
        y = y + _dot(qq, states)
        outs = []
        for j in range(v_w // LANES):
            yb = y[:, j * LANES:(j + 1) * LANES]
            lo = lax.broadcasted_iota(jnp.int32, yb.shape, 1) < C_V_DIM
            s_all = jnp.sum(yb, axis=-1, keepdims=True)
            s_lo = jnp.sum(jnp.where(lo, yb, 0.0), axis=-1, keepdims=True)
            yc = yb - jnp.where(lo, s_lo, s_all - s_lo) * (1.0 / C_V_DIM)
            sq = yc * yc
            q_all = jnp.sum(sq, axis=-1, keepdims=True)
            q_lo = jnp.sum(jnp.where(lo, sq, 0.0), axis=-1, keepdims=True)
            var = jnp.where(lo, q_lo, q_all - q_lo) * (1.0 / C_V_DIM)
            outs.append(yc * lax.rsqrt(var + NORM_EPS))
        yn = jnp.concatenate(outs, axis=1)
        gate = g_ref[0, pl.ds(off, ch), :]
        o_ref[0, pl.ds(off, ch), :] = (gate * _sigmoid(gate) * yn).astype(BF16)
        (u_b,) = kv_update(k.astype(F32), v, [k_dec_b])
        return chunk_b * state_b + u_b

    lax.fori_loop(0, nc, bwd, s_b, unroll=min(nc, 8))


def _mixer_c(prm, l, q, k, v, g, kc, vc):
    bsz, n, _ = q.shape
    ctx_len = kc.shape[1] if kc is not None else 0
    seq = lambda wd: pl.BlockSpec((1, n, wd), lambda b: (b, 0, 0))
    once = lambda wd: pl.BlockSpec((1, n, wd), lambda b: (b, 0, 0), pipeline_mode=pl.Buffered(1))
    in_specs = [pl.BlockSpec(memory_space=pltpu.SMEM), once(C_QK), once(C_QK), once(C_V), once(C_V)]
    args = [prm["decay"], q, k, v, g]
    if ctx_len:
        in_specs += [pl.BlockSpec((1, ctx_len, C_QK), lambda b: (b, 0, 0)),
                     pl.BlockSpec((1, ctx_len, C_V), lambda b: (b, 0, 0))]
        args += [kc, vc]
    return pl.pallas_call(
        functools.partial(_mixer_c_kernel, n=n, ctx_len=ctx_len, layer=l),
        grid=(bsz,),
        in_specs=in_specs,
        out_specs=seq(C_V),
        out_shape=jax.ShapeDtypeStruct((bsz, n, C_V), BF16),
        scratch_shapes=[pltpu.VMEM((n // C_CHUNK, C_QK, C_V), BF16)],
        compiler_params=_params("parallel"),
        name="mixer_c_latent" if ctx_len else "mixer_c_ctx",
    )(*args)


FF_CHUNK = 256
HALO = 2 * SUBLANES


def _post_kernel(*refs, tm, d_ff, final):
    (am, ap, an, bm, bp, bn, cm, cp, cn, xm, xp, xn_, wo_ref, g1_ref, n2_ref, sh_ref, sc_ref,
     wup_ref, cw_ref, cb_ref, wdn_ref, g2_ref, fg_ref, o_ref, mix_ref, xs_ref, hs_ref, act_ref) = refs
    i = pl.program_id(1)
    last = pl.num_programs(1) - 1
    rows = tm + 2 * HALO
    col = 0
    for main, prev, nxt in ((am, ap, an), (bm, bp, bn), (cm, cp, cn)):
        wd = main.shape[-1]
        mix_ref[0:HALO, col:col + wd] = prev[0]
        mix_ref[HALO:HALO + tm, col:col + wd] = main[0]
        mix_ref[HALO + tm:rows, col:col + wd] = nxt[0]
        col += wd
    xs_ref[0:HALO, :] = xp[0]
    xs_ref[HALO:HALO + tm, :] = xm[0]
    xs_ref[HALO + tm:rows, :] = xn_[0]

    xs_ref[...] = xs_ref[...] + g1_ref[0] * _dot(mix_ref[...], wo_ref[...])
    h = (_rms(xs_ref[...]) * n2_ref[...]) * (1.0 + sc_ref[0]) + sh_ref[0]
    hs_ref[...] = h.astype(BF16)
    zero = jnp.zeros((HALO, h.shape[1]), BF16)

    @pl.when(i == 0)
    def _():
        hs_ref[0:HALO, :] = zero

    @pl.when(i == last)
    def _():
        hs_ref[HALO + tm:rows, :] = zero

    hs = hs_ref[...]

    def conv(u, off):
        w = cw_ref[:, off:off + FF_CHUNK]
        return (u[HALO - 1:HALO - 1 + tm] * w[0:1] + u[HALO:HALO + tm] * w[1:2]
                + u[HALO + 1:HALO + 1 + tm] * w[2:3] + cb_ref[:, off:off + FF_CHUNK])

    for c in range(d_ff // FF_CHUNK):
        lo = c * FF_CHUNK
        val = conv(_dot(hs, wup_ref[:, lo:lo + FF_CHUNK]), lo)
        gate = conv(_dot(hs, wup_ref[:, d_ff + lo:d_ff + lo + FF_CHUNK]), d_ff + lo)
        act_ref[:, lo:lo + FF_CHUNK] = (gate * _sigmoid(gate) * val).astype(BF16)
    y = _dot(act_ref[...], wdn_ref[...])
    out = xs_ref[HALO:HALO + tm, :] + g2_ref[0] * y
    if final:
        out = _rms(out) * fg_ref[...]
    o_ref[0] = out


def _post(a, b, c, x, prm, l, mod_row, tm, final):
    bsz, n, d = x.shape
    d_ff = prm["w_down"].shape[1]
    nb = tm // HALO
    n_halo_blocks = n // HALO
    layer = lambda name: _layer_spec(prm[name], l, single_buffer=True)

    def tiles(wd):
        return [pl.BlockSpec((1, tm, wd), lambda bb, i: (bb, i, 0)),
                pl.BlockSpec((1, HALO, wd), lambda bb, i: (bb, jnp.maximum(i * nb - 1, 0), 0)),
                pl.BlockSpec((1, HALO, wd),
                             lambda bb, i: (bb, jnp.minimum((i + 1) * nb, n_halo_blocks - 1), 0))]

    rows = tm + 2 * HALO
    return pl.pallas_call(
        functools.partial(_post_kernel, tm=tm, d_ff=d_ff, final=final),
        grid=(bsz, n // tm),
        in_specs=(tiles(A_Q) + tiles(B_V) + tiles(C_V) + tiles(d)
                  + [layer("w_out"), mod_row(2), layer("norm2_g"), mod_row(3), mod_row(4),
                     layer("w_up"), layer("conv_w"), layer("conv_b"), layer("w_down"), mod_row(5),
                     pl.BlockSpec((1, d), lambda bb, i: (0, 0))]),
        out_specs=pl.BlockSpec((1, tm, d), lambda bb, i: (bb, i, 0)),
        out_shape=jax.ShapeDtypeStruct((bsz, n, d), F32),
        scratch_shapes=[pltpu.VMEM((rows, d), BF16), pltpu.VMEM((rows, d), F32),
                        pltpu.VMEM((rows, d), BF16), pltpu.VMEM((tm, d_ff), BF16)],
        compiler_params=_params("parallel", "parallel", vmem=POST_VMEM_LIMIT),
        name="post_final" if final else "post",
    )(a, a, a, b, b, b, c, c, c, x, x, x, prm["w_out"], prm["mod"], prm["norm2_g"], prm["mod"],
      prm["mod"], prm["w_up"], prm["conv_w"], prm["conv_b"], prm["w_down"], prm["mod"],
      prm["final_g"])


def _rope_tables(n_tok, dim):
    n_rows = n_tok // GRID_W
    rows = jnp.repeat(jnp.arange(n_rows, dtype=F32), GRID_W)
    cols = jnp.tile(jnp.arange(GRID_W, dtype=F32), n_rows)
    n_freq = dim // 4
    inv = ROPE_BASE ** (-jnp.arange(n_freq, dtype=F32) / n_freq)
    ang = jnp.concatenate([rows[:, None] * inv, cols[:, None] * inv], axis=-1)
    cos, sin = jnp.cos(ang), jnp.sin(ang)
    reps = LANES // dim
    cos_t = jnp.tile(jnp.concatenate([cos, cos], axis=-1), (1, reps))
    sin_t = jnp.tile(jnp.concatenate([-sin, sin], axis=-1), (1, reps))
    return cos_t, sin_t


def _tile_rows(n, target):
    t = min(n, target)
    while n % t:
        t //= 2
    return t


def kernel(x, c, ctx, c_ctx, w_mod, b_mod, norm1_g, norm2_g, w_in, w_out, attn_sink,
           diff_lambda, diff_subln_g, ret_decay_logit, w_up, conv_w, conv_b, w_down, final_g):
    bsz, n_lat, d = x.shape
    depth = w_mod.shape[0]
    tabs = _rope_tables(n_lat, HEAD_DIM) + _rope_tables(n_lat, C_QK_DIM)

    rows = -(-(bsz + 1) // SUBLANES) * SUBLANES
    cc = jnp.concatenate([c, c_ctx[None, :], jnp.zeros((rows - bsz - 1, d), F32)], axis=0)
    mod = _modulation(cc, w_mod, b_mod)
    mod = mod.reshape(depth * rows * N_MOD, 1, d)

    prm = dict(
        w_in=w_in.astype(BF16), w_out=w_out.astype(BF16), w_up=w_up.astype(BF16),
        w_down=w_down.astype(BF16), conv_w=conv_w, conv_b=conv_b[:, None, :],
        norm1_g=norm1_g[:, None, :], norm2_g=norm2_g[:, None, :],
        subln_g=diff_subln_g[:, None, :], lam=diff_lambda, sink=attn_sink,
        decay=ret_decay_logit, final_g=final_g.reshape(1, d), mod=mod)

    tm_lat = _tile_rows(n_lat, 1024)
    tm_ctx = _tile_rows(ctx.shape[1], 1024)
    xc = ctx
    for l in range(depth):
        is_last = l == depth - 1
        lam_init = 0.8 - 0.6 * math.exp(-0.3 * l)

        def mod_row(j, is_ctx, l=l):
            if is_ctx:
                return pl.BlockSpec((1, 1, d), lambda b, *_: ((l * rows + bsz) * N_MOD + j, 0, 0))
            return pl.BlockSpec((1, 1, d), lambda b, *_: ((l * rows + b) * N_MOD + j, 0, 0))

        lat = functools.partial(mod_row, is_ctx=False)
        cmod = functools.partial(mod_row, is_ctx=True)

        (qa, ka, va, qb, kb, vb, qr, kr, vr, gr) = _in_proj(x, prm, l, lat, tabs, tm_lat)
        (qa_c, ka_c, va_c, qb_c, kb_c, vb_c, qr_c, kr_c, vr_c, gr_c) = _in_proj(
            xc, prm, l, cmod, None, tm_ctx)

        a_out = _mixer_a(prm, l, qa, ka, va, ka_c, va_c, tm_lat)
        b_out = _mixer_b(prm, l, qb, kb_c, vb_c, kb, vb, lam_init, tm_lat, _tile_rows(n_lat, 512))
        c_out = _mixer_c(prm, l, qr, kr, vr, gr, kr_c, vr_c)
        x = _post(a_out, b_out, c_out, x, prm, l, lat, tm_lat, is_last)

        if not is_last:
            a_c = _mixer_a(prm, l, qa_c, None, None, ka_c, va_c, tm_ctx)
            b_c = _mixer_b(prm, l, qb_c, kb_c, vb_c, None, None, lam_init, tm_ctx, tm_ctx)
            c_c = _mixer_c(prm, l, qr_c, kr_c, vr_c, gr_c, None, None)
            xc = _post(a_c, b_c, c_c, xc, prm, l, cmod, tm_ctx, False)
    return x
```

```python
import functools
import math

import jax
import jax.numpy as jnp
from jax import lax
from jax.experimental import pallas as pl
from jax.experimental.pallas import tpu as pltpu

F32 = jnp.float32
BF16 = jnp.bfloat16

GRID_W = 64
HEAD_DIM = 64
ROPE_BASE = 10000.0
NORM_EPS = 1e-6
NEG_INF = -1e30

A_HEADS = 4
A_KV_HEADS = 2
A_WINDOW = 128
A_BLOCK = 128
B_HEADS = 4
B_QK_DIM = 64
B_V_DIM = 128
C_HEADS = 4
C_QK_DIM = 32
C_V_DIM = 64
C_CHUNK = 128

A_Q = A_HEADS * HEAD_DIM
A_KV = A_KV_HEADS * HEAD_DIM
B_QK = B_HEADS * 2 * B_QK_DIM
B_V = B_HEADS * B_V_DIM
C_QK = C_HEADS * C_QK_DIM
C_V = C_HEADS * C_V_DIM
IN_WIDTH = A_Q + 2 * A_KV + 2 * B_QK + B_V + 2 * C_QK + 2 * C_V
N_MOD = 6
CONV_W = 3

assert A_WINDOW == A_BLOCK

LANES = 128
SUBLANES = 8
VMEM_LIMIT = 56 * 1024 * 1024
POST_VMEM_LIMIT = 62 * 1024 * 1024

OFF_QA = 0
OFF_QB = A_Q + 2 * A_KV
OFF_KB = OFF_QB + B_QK
OFF_VB = OFF_KB + B_QK
OFF_QR = OFF_VB + B_V
OFF_VR = OFF_QR + 2 * C_QK
OFF_GR = OFF_VR + C_V


def _params(*sem, vmem=VMEM_LIMIT):
    return pltpu.CompilerParams(dimension_semantics=sem, vmem_limit_bytes=vmem)


def _layer_spec(arr, l, single_buffer=False):
    zeros = (0,) * (arr.ndim - 1)
    mode = dict(pipeline_mode=pl.Buffered(1)) if single_buffer else {}
    return pl.BlockSpec((None,) + arr.shape[1:], lambda *_: (l,) + zeros, **mode)


def _nt_dot(a, b):
    return lax.dot_general(a, b, (((1,), (1,)), ((), ())), preferred_element_type=F32)


def _dot(a, b):
    return jnp.dot(a, b, preferred_element_type=F32)


def _rms(x):
    return x * lax.rsqrt(jnp.mean(x * x, axis=-1, keepdims=True) + NORM_EPS)


def _sigmoid(x):
    return jax.nn.sigmoid(x)


def _mod_kernel(c_ref, w_ref, b_ref, o_ref):
    c = c_ref[...]
    a = (c * _sigmoid(c)).astype(BF16)
    o_ref[0] = _dot(a, w_ref[0].astype(BF16)) + b_ref[0]


def _modulation(cc, w_mod, b_mod):
    depth, d, width = w_mod.shape
    rows = cc.shape[0]
    blk = d
    return pl.pallas_call(
        _mod_kernel,
        grid=(depth, width // blk),
        in_specs=[
            pl.BlockSpec((rows, d), lambda l, j: (0, 0)),
            pl.BlockSpec((1, d, blk), lambda l, j: (l, 0, j)),
            pl.BlockSpec((1, 1, blk), lambda l, j: (l, 0, j)),
        ],
        out_specs=pl.BlockSpec((1, rows, blk), lambda l, j: (l, 0, j)),
        out_shape=jax.ShapeDtypeStruct((depth, rows, width), F32),
        compiler_params=_params("arbitrary", "arbitrary"),
        name="modulation",
    )(cc, w_mod, b_mod.reshape(depth, 1, width))


def _rope_block(blk, cos, sin_signed, half):
    lane = lax.broadcasted_iota(jnp.int32, blk.shape, 1)
    first = (lane & (2 * half - 1)) < half
    swapped = jnp.where(first, pltpu.roll(blk, LANES - half, 1), pltpu.roll(blk, half, 1))
    return blk * cos + swapped * sin_signed


def _inproj_kernel(*refs, rope):
    if rope:
        (x_ref, g_ref, sh_ref, sc_ref, w_ref, c64_ref, s64_ref, c32_ref, s32_ref,
         qa_ref, ka_ref, va_ref, qb_ref, kb_ref, vb_ref, qr_ref, kr_ref, vr_ref, gr_ref) = refs
    else:
        (x_ref, g_ref, sh_ref, sc_ref, w_ref,
         qa_ref, ka_ref, va_ref, qb_ref, kb_ref, vb_ref, qr_ref, kr_ref, vr_ref, gr_ref) = refs
    x = x_ref[0]
    h = (_rms(x) * g_ref[...]) * (1.0 + sc_ref[0]) + sh_ref[0]
    h = h.astype(BF16)

    def rot(blk, dim):
        if not rope:
            return blk
        if dim == HEAD_DIM:
            return _rope_block(blk, c64_ref[...], s64_ref[...], HEAD_DIM // 2)
        return _rope_block(blk, c32_ref[...], s32_ref[...], C_QK_DIM // 2)

    a_scale = HEAD_DIM ** -0.5 * math.log2(math.e)
    b_scale = B_QK_DIM ** -0.5 * math.log2(math.e)
    c_scale = C_QK_DIM ** -0.5

    p = _dot(h, w_ref[:, OFF_QA:OFF_QB])
    for j in range(A_Q // LANES):
        qa_ref[0, :, j * LANES:(j + 1) * LANES] = rot(
            p[:, j * LANES:(j + 1) * LANES] * a_scale, HEAD_DIM).astype(BF16)
    ka_ref[0] = rot(p[:, A_Q:A_Q + A_KV], HEAD_DIM).astype(BF16)
    va_ref[0, :, :A_KV] = p[:, A_Q + A_KV:A_Q + 2 * A_KV].astype(BF16)
    va_ref[0, :, A_KV:] = jnp.ones((p.shape[0], A_KV), BF16)

    p = _dot(h, w_ref[:, OFF_QB:OFF_KB])
    for j in range(B_QK // LANES):
        qb_ref[0, :, j * LANES:(j + 1) * LANES] = rot(
            p[:, j * LANES:(j + 1) * LANES] * b_scale, HEAD_DIM).astype(BF16)
    p = _dot(h, w_ref[:, OFF_KB:OFF_VB])
    for j in range(B_QK // LANES):
        kb_ref[0, :, j * LANES:(j + 1) * LANES] = rot(
            p[:, j * LANES:(j + 1) * LANES], HEAD_DIM).astype(BF16)
    p = _dot(h, w_ref[:, OFF_VB:OFF_QR]).astype(BF16)
    for j in range(B_HEADS):
        vb_ref[0, :, 2 * j * B_V_DIM:(2 * j + 1) * B_V_DIM] = p[:, j * B_V_DIM:(j + 1) * B_V_DIM]
        vb_ref[0, :, (2 * j + 1) * B_V_DIM:(2 * j + 2) * B_V_DIM] = jnp.ones(
            (p.shape[0], B_V_DIM), BF16)

    p = _dot(h, w_ref[:, OFF_QR:OFF_VR])
    qr_ref[0] = rot(p[:, :C_QK], C_QK_DIM).astype(BF16)
    kr_ref[0] = rot(p[:, C_QK:] * c_scale, C_QK_DIM).astype(BF16)
    vr_ref[0] = _dot(h, w_ref[:, OFF_VR:OFF_GR]).astype(BF16)
    gr_ref[0] = _dot(h, w_ref[:, OFF_GR:IN_WIDTH])


def _in_proj(x, prm, l, mod_row, tabs, tm):
    bsz, n, d = x.shape
    rope = tabs is not None
    in_specs = [
        pl.BlockSpec((1, tm, d), lambda b, i: (b, i, 0)),
        _layer_spec(prm["norm1_g"], l),
        mod_row(0),
        mod_row(1),
        _layer_spec(prm["w_in"], l),
    ]
    args = [x, prm["norm1_g"], prm["mod"], prm["mod"], prm["w_in"]]
    if rope:
        in_specs += [pl.BlockSpec((tm, LANES), lambda b, i: (i, 0))] * 4
        args += list(tabs)
    widths = (A_Q, A_KV, 2 * A_KV, B_QK, B_QK, 2 * B_V, C_QK, C_QK, C_V, C_V)
    dtypes = (BF16,) * 9 + (F32,)
    out_specs = [pl.BlockSpec((1, tm, wd), lambda b, i: (b, i, 0)) for wd in widths]
    out_shape = [jax.ShapeDtypeStruct((bsz, n, wd), dt) for wd, dt in zip(widths, dtypes)]
    return pl.pallas_call(
        functools.partial(_inproj_kernel, rope=rope),
        grid=(bsz, n // tm),
        in_specs=in_specs,
        out_specs=out_specs,
        out_shape=out_shape,
        compiler_params=_params("parallel", "parallel"),
        name="in_proj_rope" if rope else "in_proj_ctx",
    )(*args)


def _mixer_a_kernel(*refs, tq, n_lat, local, layer):
    if local:
        sink_ref, q_ref, k_ref, v_ref, kc_ref, vc_ref, o_ref = refs
    else:
        sink_ref, q_ref, kc_ref, vc_ref, o_ref = refs
    blk = A_BLOCK
    half = HEAD_DIM
    lane = lax.broadcasted_iota(jnp.int32, (blk, LANES), 1)
    low = lane < half
    rows = A_HEADS * blk
    row_head = lax.broadcasted_iota(jnp.int32, (rows, 1), 0) // blk
    sink = jnp.zeros((rows, 1), F32)
    for h in range(A_HEADS):
        sink = jnp.where(row_head == h, sink_ref[layer, h], sink)
    sink = sink * math.log2(math.e)
    kc = kc_ref[0]
    vc = vc_ref[0]
    if local:
        nb = n_lat // blk
        qrow = lax.broadcasted_iota(jnp.int32, (rows, blk), 0) & (blk - 1)
        kcol = lax.broadcasted_iota(jnp.int32, (rows, blk), 1)
        tri_prev = kcol >= qrow
        tri_next = kcol <= qrow
    for sub in range(tq // blk):
        r0 = sub * blk
        gb = pl.program_id(1) * (tq // blk) + sub
        if local:
            prev0 = pl.multiple_of(jnp.maximum(gb - 1, 0) * blk, blk)
            cur0 = pl.multiple_of(gb * blk, blk)
            next0 = pl.multiple_of(jnp.minimum(gb + 1, nb - 1) * blk, blk)
            k_all = jnp.concatenate([k_ref[0, pl.ds(prev0, blk), :], k_ref[0, pl.ds(cur0, blk), :],
                                     k_ref[0, pl.ds(next0, blk), :], kc], axis=0)
            v_all = jnp.concatenate([v_ref[0, pl.ds(prev0, blk), :], v_ref[0, pl.ds(cur0, blk), :],
                                     v_ref[0, pl.ds(next0, blk), :], vc], axis=0)
            ok_prev = jnp.logical_and(tri_prev, gb > 0)
            ok_next = jnp.logical_and(tri_next, gb < nb - 1)
        else:
            k_all, v_all = kc, vc
        qs = []
        for hk in range(A_KV_HEADS):
            qp = q_ref[0, r0:r0 + blk, hk * LANES:(hk + 1) * LANES].astype(F32)
            qsw = pltpu.roll(qp, half, 1)
            if hk == 0:
                qs += [jnp.where(low, qp, 0.0), jnp.where(low, qsw, 0.0)]
            else:
                qs += [jnp.where(low, 0.0, qsw), jnp.where(low, 0.0, qp)]
        qz = jnp.concatenate(qs, axis=0).astype(BF16)
        s = _nt_dot(qz, k_all)
        if local:
            s = jnp.concatenate([jnp.where(ok_prev, s[:, :blk], NEG_INF), s[:, blk:2 * blk],
                                 jnp.where(ok_next, s[:, 2 * blk:3 * blk], NEG_INF),
                                 s[:, 3 * blk:]], axis=1)
        m = jnp.maximum(jnp.max(s, axis=-1, keepdims=True), sink)
        r = _dot(jnp.exp2(s - m).astype(BF16), v_all)
        r = r[:, :LANES] / (r[:, LANES:] + jnp.exp2(sink - m))
        o_ref[0, r0:r0 + blk, :LANES] = jnp.where(
            low, r[:blk], pltpu.roll(r[blk:2 * blk], half, 1)).astype(BF16)
        o_ref[0, r0:r0 + blk, LANES:] = jnp.where(
            low, pltpu.roll(r[2 * blk:3 * blk], half, 1), r[3 * blk:]).astype(BF16)


def _mixer_a(prm, l, q, k, v, kc, vc, tq):
    bsz, n, _ = q.shape
    ctx_len = kc.shape[1]
    local = k is not None
    in_specs = [pl.BlockSpec(memory_space=pltpu.SMEM),
                pl.BlockSpec((1, tq, A_Q), lambda b, i: (b, i, 0))]
    args = [prm["sink"], q]
    if local:
        in_specs += [pl.BlockSpec((1, n, A_KV), lambda b, i: (b, 0, 0)),
                     pl.BlockSpec((1, n, 2 * A_KV), lambda b, i: (b, 0, 0))]
        args += [k, v]
    in_specs += [pl.BlockSpec((1, ctx_len, A_KV), lambda b, i: (b, 0, 0)),
                 pl.BlockSpec((1, ctx_len, 2 * A_KV), lambda b, i: (b, 0, 0))]
    args += [kc, vc]
    return pl.pallas_call(
        functools.partial(_mixer_a_kernel, tq=tq, n_lat=n, local=local, layer=l),
        grid=(bsz, n // tq),
        in_specs=in_specs,
        out_specs=pl.BlockSpec((1, tq, A_Q), lambda b, i: (b, i, 0)),
        out_shape=jax.ShapeDtypeStruct((bsz, n, A_Q), BF16),
        compiler_params=_params("parallel", "parallel"),
        name="mixer_a_window" if local else "mixer_a_ctx",
    )(*args)


def _mixer_b_kernel(*refs, tq, tk, n_lat, lam_init):
    if n_lat:
        lam_ref, g_ref, q_ref, kc_ref, vc_ref, k_ref, v_ref, o_ref, acc_ref = refs
    else:
        lam_ref, g_ref, q_ref, kc_ref, vc_ref, o_ref, acc_ref = refs
    q = q_ref[0]
    lane = lax.broadcasted_iota(jnp.int32, q.shape, 1)
    zero = jnp.zeros_like(q)
    qz = (jnp.where(lane < B_QK_DIM, q, zero), jnp.where(lane < B_QK_DIM, zero, q))

    def step(kblk, vblk, ms, first):
        out = []
        for c in range(2):
            s = _nt_dot(qz[c], kblk)
            m_new = jnp.maximum(ms[c], jnp.max(s, axis=-1, keepdims=True))
            p = jnp.exp2(s - m_new)
            pv = _dot(p.astype(BF16), vblk)
            if first:
                acc_ref[c] = pv
            else:
                acc_ref[c] = jnp.exp2(ms[c] - m_new) * acc_ref[c] + pv
            out.append(m_new)
        return tuple(out)

    m0 = jnp.full((tq, 1), NEG_INF, F32)
    ms = step(kc_ref[0], vc_ref[0], (m0, m0), True)
    for j in range(n_lat // tk):
        ms = step(k_ref[0, j * tk:(j + 1) * tk, :], v_ref[0, j * tk:(j + 1) * tk, :], ms, False)

    d = lam_ref[...]
    lam = (jnp.exp(jnp.sum(d[0:1] * d[1:2], axis=-1, keepdims=True))
           - jnp.exp(jnp.sum(d[2:3] * d[3:4], axis=-1, keepdims=True)) + lam_init)
    o = (acc_ref[0, :, :B_V_DIM] / acc_ref[0, :, B_V_DIM:]
         - lam * (acc_ref[1, :, :B_V_DIM] / acc_ref[1, :, B_V_DIM:]))
    y = _rms(o) * g_ref[...] * (1.0 - lam_init)
    o_ref[0] = y.astype(BF16)


def _mixer_b(prm, l, q, kc, vc, k, v, lam_init, tq, tk):
    bsz, n, _ = q.shape
    ctx_len = kc.shape[1]
    has_lat = k is not None
    n_lat = k.shape[1] if has_lat else 0
    in_specs = [
        _layer_spec(prm["lam"], l),
        _layer_spec(prm["subln_g"], l),
        pl.BlockSpec((1, tq, LANES), lambda b, h, i: (b, i, h)),
        pl.BlockSpec((1, ctx_len, LANES), lambda b, h, i: (b, 0, h)),
        pl.BlockSpec((1, ctx_len, 2 * B_V_DIM), lambda b, h, i: (b, 0, h)),
    ]
    args = [prm["lam"], prm["subln_g"], q, kc, vc]
    if has_lat:
        in_specs += [pl.BlockSpec((1, n_lat, LANES), lambda b, h, i: (b, 0, h)),
                     pl.BlockSpec((1, n_lat, 2 * B_V_DIM), lambda b, h, i: (b, 0, h))]
        args += [k, v]
    return pl.pallas_call(
        functools.partial(_mixer_b_kernel, tq=tq, tk=tk, n_lat=n_lat, lam_init=lam_init),
        grid=(bsz, B_HEADS, n // tq),
        in_specs=in_specs,
        out_specs=pl.BlockSpec((1, tq, LANES), lambda b, h, i: (b, i, h)),
        out_shape=jax.ShapeDtypeStruct((bsz, n, B_V), BF16),
        scratch_shapes=[pltpu.VMEM((2, tq, 2 * B_V_DIM), F32)],
        compiler_params=_params("parallel", "parallel", "parallel"),
        name="mixer_b_latent" if has_lat else "mixer_b_ctx",
    )(*args)


def _log_sigmoid(x):
    return jnp.minimum(x, 0.0) - jnp.log(1.0 + jnp.exp(-jnp.abs(x)))


def _per_head(ref, row, shape, axis, width):
    idx = lax.broadcasted_iota(jnp.int32, shape, axis) // width
    out = jnp.zeros(shape, F32)
    for h in range(C_HEADS):
        out = jnp.where(idx == h, ref[row + (h,)], out)
    return out


def _mixer_c_kernel(*refs, n, ctx_len, layer):
    if ctx_len:
        dec_ref, q_ref, k_ref, v_ref, g_ref, kc_ref, vc_ref, o_ref, sf_ref = refs
    else:
        dec_ref, q_ref, k_ref, v_ref, g_ref, o_ref, sf_ref = refs
    ch = C_CHUNK
    nc = n // ch
    qk_w, v_w = C_QK, C_V

    fwd_row, bwd_row = (layer, 0), (layer, 1)
    lgf_qk = _log_sigmoid(_per_head(dec_ref, fwd_row, (1, qk_w), 1, C_QK_DIM))
    lgb_qk = _log_sigmoid(_per_head(dec_ref, bwd_row, (1, qk_w), 1, C_QK_DIM))
    lgf_blk = _log_sigmoid(_per_head(dec_ref, fwd_row, (1, C_HEADS * ch), 1, ch))
    lgb_blk = _log_sigmoid(_per_head(dec_ref, bwd_row, (1, C_HEADS * ch), 1, ch))
    lgf_row = _log_sigmoid(_per_head(dec_ref, fwd_row, (qk_w, 1), 0, C_QK_DIM))
    lgb_row = _log_sigmoid(_per_head(dec_ref, bwd_row, (qk_w, 1), 0, C_QK_DIM))

    pos = lax.broadcasted_iota(jnp.int32, (ch, 1), 0).astype(F32)
    q_dec_f = jnp.exp(lgf_qk * (pos + 1.0))
    q_dec_b = jnp.exp(lgb_qk * (ch - pos))
    k_dec_f = jnp.exp(lgf_qk * (ch - 1.0 - pos))
    k_dec_b = jnp.exp(lgb_qk * pos)
    chunk_f = jnp.exp(lgf_row * float(ch))
    chunk_b = jnp.exp(lgb_row * float(ch))
    col = lax.broadcasted_iota(jnp.int32, (ch, C_HEADS * ch), 1) & (ch - 1)
    diff = (lax.broadcasted_iota(jnp.int32, (ch, C_HEADS * ch), 0) - col).astype(F32)
    decay = jnp.where(diff >= 0.0, jnp.exp(lgf_blk * jnp.maximum(diff, 0.0)),
                      jnp.exp(lgb_blk * jnp.maximum(-diff, 0.0)))
    state_mask = (lax.broadcasted_iota(jnp.int32, (qk_w, v_w), 0) // C_QK_DIM
                  == lax.broadcasted_iota(jnp.int32, (qk_w, v_w), 1) // C_V_DIM)
    head_qk = lax.broadcasted_iota(jnp.int32, (ch, qk_w), 1) // C_QK_DIM
    head_v = lax.broadcasted_iota(jnp.int32, (ch, v_w), 1) // C_V_DIM

    def kv_update(kf32, vb16, weights):
        kk = jnp.concatenate([kf32 * w for w in weights], axis=1) if len(weights) > 1 \
            else kf32 * weights[0]
        u = _dot(kk.T.astype(BF16), vb16)
        return [jnp.where(state_mask, u[j * qk_w:(j + 1) * qk_w], 0.0) for j in range(len(weights))]

    if ctx_len:
        cpos = lax.broadcasted_iota(jnp.int32, (ctx_len, 1), 0).astype(F32)
        s_f, s_b = kv_update(kc_ref[0].astype(F32), vc_ref[0],
                             [jnp.exp(lgf_qk * (ctx_len - 1.0 - cpos)), jnp.exp(lgb_qk * cpos)])
    else:
        s_f = jnp.zeros((qk_w, v_w), F32)
        s_b = jnp.zeros((qk_w, v_w), F32)

    def fwd(c, state):
        off = pl.multiple_of(c * ch, ch)
        sf_ref[c] = state.astype(BF16)
        (u_f,) = kv_update(k_ref[0, pl.ds(off, ch), :].astype(F32), v_ref[0, pl.ds(off, ch), :],
                           [k_dec_f])
        return chunk_f * state + u_f

    lax.fori_loop(0, nc, fwd, s_f, unroll=min(nc, 16))

    def bwd(t, state_b):
        c = nc - 1 - t
        off = pl.multiple_of(c * ch, ch)
        q = q_ref[0, pl.ds(off, ch), :]
        k = k_ref[0, pl.ds(off, ch), :]
        v = v_ref[0, pl.ds(off, ch), :]
        zq = jnp.zeros_like(k)
        zv = jnp.zeros_like(v)
        kz = jnp.concatenate([jnp.where(head_qk == h, k, zq) for h in range(C_HEADS)], axis=0)
        vz = jnp.concatenate([jnp.where(head_v == h, v, zv) for h in range(C_HEADS)], axis=0)
        att = (_nt_dot(q, kz) * decay).astype(BF16)
        y = _dot(att, vz)
        qf = q.astype(F32)
        qq = jnp.concatenate([qf * q_dec_f, qf * q_dec_b], axis=1).astype(BF16)
        states = jnp.concatenate([sf_ref[c], state_b.astype(BF16)], axis=0)
        y = y + _dot(qq, states)
        outs = []
        for j in range(v_w // LANES):
            yb = y[:, j * LANES:(j + 1) * LANES]
            lo = lax.broadcasted_iota(jnp.int32, yb.shape, 1) < C_V_DIM
            s_all = jnp.sum(yb, axis=-1, keepdims=True)
            s_lo = jnp.sum(jnp.where(lo, yb, 0.0), axis=-1, keepdims=True)
            yc = yb - jnp.where(lo, s_lo, s_all - s_lo) * (1.0 / C_V_DIM)
            sq = yc * yc
            q_all = jnp.sum(sq, axis=-1, keepdims=True)
            q_lo = jnp.sum(jnp.where(lo, sq, 0.0), axis=-1, keepdims=True)
            var = jnp.where(lo, q_lo, q_all - q_lo) * (1.0 / C_V_DIM)
            outs.append(yc * lax.rsqrt(var + NORM_EPS))
        yn = jnp.concatenate(outs, axis=1)
        gate = g_ref[0, pl.ds(off, ch), :]
        o_ref[0, pl.ds(off, ch), :] = (gate * _sigmoid(gate) * yn).astype(BF16)
        (u_b,) = kv_update(k.astype(F32), v, [k_dec_b])
        return chunk_b * state_b + u_b

    lax.fori_loop(0, nc, bwd, s_b, unroll=min(nc, 8))


def _mixer_c(prm, l, q, k, v, g, kc, vc):
    bsz, n, _ = q.shape
    ctx_len = kc.shape[1] if kc is not None else 0
    seq = lambda wd: pl.BlockSpec((1, n, wd), lambda b: (b, 0, 0))
    once = lambda wd: pl.BlockSpec((1, n, wd), lambda b: (b, 0, 0), pipeline_mode=pl.Buffered(1))
    in_specs = [pl.BlockSpec(memory_space=pltpu.SMEM), once(C_QK), once(C_QK), once(C_V), once(C_V)]
    args = [prm["decay"], q, k, v, g]
    if ctx_len:
        in_specs += [pl.BlockSpec((1, ctx_len, C_QK), lambda b: (b, 0, 0)),
                     pl.BlockSpec((1, ctx_len, C_V), lambda b: (b, 0, 0))]
        args += [kc, vc]
    return pl.pallas_call(
        functools.partial(_mixer_c_kernel, n=n, ctx_len=ctx_len, layer=l),
        grid=(bsz,),
        in_specs=in_specs,
        out_specs=seq(C_V),
        out_shape=jax.ShapeDtypeStruct((bsz, n, C_V), BF16),
        scratch_shapes=[pltpu.VMEM((n // C_CHUNK, C_QK, C_V), BF16)],
        compiler_params=_params("parallel"),
        name="mixer_c_latent" if ctx_len else "mixer_c_ctx",
    )(*args)


FF_CHUNK = 256
HALO = 2 * SUBLANES


def _post_kernel(*refs, tm, d_ff, final):
    (am, ap, an, bm, bp, bn, cm, cp, cn, xm, xp, xn_, wo_ref, g1_ref, n2_ref, sh_ref, sc_ref,
     wup_ref, cw_ref, cb_ref, wdn_ref, g2_ref, fg_ref, o_ref, mix_ref, xs_ref, hs_ref, act_ref) = refs
    i = pl.program_id(1)
    last = pl.num_programs(1) - 1
    rows = tm + 2 * HALO
    col = 0
    for main, prev, nxt in ((am, ap, an), (bm, bp, bn), (cm, cp, cn)):
        wd = main.shape[-1]
        mix_ref[0:HALO, col:col + wd] = prev[0]
        mix_ref[HALO:HALO + tm, col:col + wd] = main[0]
        mix_ref[HALO + tm:rows, col:col + wd] = nxt[0]
        col += wd
    xs_ref[0:HALO, :] = xp[0]
    xs_ref[HALO:HALO + tm, :] = xm[0]
    xs_ref[HALO + tm:rows, :] = xn_[0]

    xs_ref[...] = xs_ref[...] + g1_ref[0] * _dot(mix_ref[...], wo_ref[...])
    h = (_rms(xs_ref[...]) * n2_ref[...]) * (1.0 + sc_ref[0]) + sh_ref[0]
    hs_ref[...] = h.astype(BF16)
    zero = jnp.zeros((HALO, h.shape[1]), BF16)

    @pl.when(i == 0)
    def _():
        hs_ref[0:HALO, :] = zero

    @pl.when(i == last)
    def _():
        hs_ref[HALO + tm:rows, :] = zero

    hs = hs_ref[...]

    def conv(u, off):
        w = cw_ref[:, off:off + FF_CHUNK]
        return (u[HALO - 1:HALO - 1 + tm] * w[0:1] + u[HALO:HALO + tm] * w[1:2]
                + u[HALO + 1:HALO + 1 + tm] * w[2:3] + cb_ref[:, off:off + FF_CHUNK])

    for c in range(d_ff // FF_CHUNK):
        lo = c * FF_CHUNK
        val = conv(_dot(hs, wup_ref[:, lo:lo + FF_CHUNK]), lo)
        gate = conv(_dot(hs, wup_ref[:, d_ff + lo:d_ff + lo + FF_CHUNK]), d_ff + lo)
        act_ref[:, lo:lo + FF_CHUNK] = (gate * _sigmoid(gate) * val).astype(BF16)
    y = _dot(act_ref[...], wdn_ref[...])
    out = xs_ref[HALO:HALO + tm, :] + g2_ref[0] * y
    if final:
        out = _rms(out) * fg_ref[...]
    o_ref[0] = out


def _post(a, b, c, x, prm, l, mod_row, tm, final):
    bsz, n, d = x.shape
    d_ff = prm["w_down"].shape[1]
    nb = tm // HALO
    n_halo_blocks = n // HALO
    layer = lambda name: _layer_spec(prm[name], l, single_buffer=True)

    def tiles(wd):
        return [pl.BlockSpec((1, tm, wd), lambda bb, i: (bb, i, 0)),
                pl.BlockSpec((1, HALO, wd), lambda bb, i: (bb, jnp.maximum(i * nb - 1, 0), 0)),
                pl.BlockSpec((1, HALO, wd),
                             lambda bb, i: (bb, jnp.minimum((i + 1) * nb, n_halo_blocks - 1), 0))]

    rows = tm + 2 * HALO
    return pl.pallas_call(
        functools.partial(_post_kernel, tm=tm, d_ff=d_ff, final=final),
        grid=(bsz, n // tm),
        in_specs=(tiles(A_Q) + tiles(B_V) + tiles(C_V) + tiles(d)
                  + [layer("w_out"), mod_row(2), layer("norm2_g"), mod_row(3), mod_row(4),
                     layer("w_up"), layer("conv_w"), layer("conv_b"), layer("w_down"), mod_row(5),
                     pl.BlockSpec((1, d), lambda bb, i: (0, 0))]),
        out_specs=pl.BlockSpec((1, tm, d), lambda bb, i: (bb, i, 0)),
        out_shape=jax.ShapeDtypeStruct((bsz, n, d), F32),
        scratch_shapes=[pltpu.VMEM((rows, d), BF16), pltpu.VMEM((rows, d), F32),
                        pltpu.VMEM((rows, d), BF16), pltpu.VMEM((tm, d_ff), BF16)],
        compiler_params=_params("parallel", "parallel", vmem=POST_VMEM_LIMIT),
        name="post_final" if final else "post",
    )(a, a, a, b, b, b, c, c, c, x, x, x, prm["w_out"], prm["mod"], prm["norm2_g"], prm["mod"],
      prm["mod"], prm["w_up"], prm["conv_w"], prm["conv_b"], prm["w_down"], prm["mod"],
      prm["final_g"])


def _rope_tables(n_tok, dim):
    n_rows = n_tok // GRID_W
    rows = jnp.repeat(jnp.arange(n_rows, dtype=F32), GRID_W)
    cols = jnp.tile(jnp.arange(GRID_W, dtype=F32), n_rows)
    n_freq = dim // 4
    inv = ROPE_BASE ** (-jnp.arange(n_freq, dtype=F32) / n_freq)
    ang = jnp.concatenate([rows[:, None] * inv, cols[:, None] * inv], axis=-1)
    cos, sin = jnp.cos(ang), jnp.sin(ang)
    reps = LANES // dim
    cos_t = jnp.tile(jnp.concatenate([cos, cos], axis=-1), (1, reps))
    sin_t = jnp.tile(jnp.concatenate([-sin, sin], axis=-1), (1, reps))
    return cos_t, sin_t


def _tile_rows(n, target):
    t = min(n, target)
    while n % t:
        t //= 2
    return t


def kernel(x, c, ctx, c_ctx, w_mod, b_mod, norm1_g, norm2_g, w_in, w_out, attn_sink,
           diff_lambda, diff_subln_g, ret_decay_logit, w_up, conv_w, conv_b, w_down, final_g):
    bsz, n_lat, d = x.shape
    depth = w_mod.shape[0]
    tabs = _rope_tables(n_lat, HEAD_DIM) + _rope_tables(n_lat, C_QK_DIM)

    rows = -(-(bsz + 1) // SUBLANES) * SUBLANES
    cc = jnp.concatenate([c, c_ctx[None, :], jnp.zeros((rows - bsz - 1, d), F32)], axis=0)
    mod = _modulation(cc, w_mod, b_mod)
    mod = mod.reshape(depth * rows * N_MOD, 1, d)

    prm = dict(
        w_in=w_in.astype(BF16), w_out=w_out.astype(BF16), w_up=w_up.astype(BF16),
        w_down=w_down.astype(BF16), conv_w=conv_w, conv_b=conv_b[:, None, :],
        norm1_g=norm1_g[:, None, :], norm2_g=norm2_g[:, None, :],
        subln_g=diff_subln_g[:, None, :], lam=diff_lambda, sink=attn_sink,
        decay=ret_decay_logit, final_g=final_g.reshape(1, d), mod=mod)

    tm_lat = _tile_rows(n_lat, 1024)
    tm_ctx = _tile_rows(ctx.shape[1], 1024)
    xc = ctx
    for l in range(depth):
        is_last = l == depth - 1
        lam_init = 0.8 - 0.6 * math.exp(-0.3 * l)

        def mod_row(j, is_ctx, l=l):
            if is_ctx:
                return pl.BlockSpec((1, 1, d), lambda b, *_: ((l * rows + bsz) * N_MOD + j, 0, 0))
            return pl.BlockSpec((1, 1, d), lambda b, *_: ((l * rows + b) * N_MOD + j, 0, 0))

        lat = functools.partial(mod_row, is_ctx=False)
        cmod = functools.partial(mod_row, is_ctx=True)

        (qa, ka, va, qb, kb, vb, qr, kr, vr, gr) = _in_proj(x, prm, l, lat, tabs, tm_lat)
        (qa_c, ka_c, va_c, qb_c, kb_c, vb_c, qr_c, kr_c, vr_c, gr_c) = _in_proj(
            xc, prm, l, cmod, None, tm_ctx)

        a_out = _mixer_a(prm, l, qa, ka, va, ka_c, va_c, tm_lat)
        b_out = _mixer_b(prm, l, qb, kb_c, vb_c, kb, vb, lam_init, tm_lat, _tile_rows(n_lat, 256))
        c_out = _mixer_c(prm, l, qr, kr, vr, gr, kr_c, vr_c)
        x = _post(a_out, b_out, c_out, x, prm, l, lat, tm_lat, is_last)

        if not is_last:
            a_c = _mixer_a(prm, l, qa_c, None, None, ka_c, va_c, tm_ctx)
            b_c = _mixer_b(prm, l, qb_c, kb_c, vb_c, None, None, lam_init, tm_ctx, tm_ctx)
            c_c = _mixer_c(prm, l, qr_c, kr_c, vr_c, gr_c, None, None)
            xc = _post(a_c, b_c, c_c, xc, prm, l, cmod, tm_ctx, False)
    return x
```
